```python
import math
import jax, jax.numpy as jnp
from jax import lax
import numpy as np

D_MODEL = 1024
BATCH = 2
SEQ = 8192
DEPTH = 2
DEC_BATCH = 32
DEC_SEQ = 32
PAST_LEN = 4096

CHUNK = 64
PLE_DIM = 256
D_FF = 2816
EPS = 1e-6
N_EVEN = (DEPTH + 1) // 2
N_ODD = DEPTH // 2
HEAD_DIM = 64

SSD_HEADS = 16
SSD_HEAD_DIM = 64
SSD_D_INNER = SSD_HEADS * SSD_HEAD_DIM
SSD_GROUPS = 2
SSD_STATE = 64
SSD_CONV = 4
SSD_CONV_DIM = SSD_D_INNER + 2 * SSD_GROUPS * SSD_STATE
SSD_BLOCK = CHUNK

BAND_HEADS = 8
BAND_WIDTH = BAND_HEADS * HEAD_DIM
BAND_CHUNKS = 8
BAND_PAST = BAND_CHUNKS * CHUNK
BAND_LEN = (BAND_CHUNKS + 1) * CHUNK
REL_CLIP = 128

FOX_HEADS = 8
FOX_WIDTH = FOX_HEADS * HEAD_DIM
Q_BLOCK = 128

MLSTM_HEADS = 4
MLSTM_HEAD_DIM = 128
MLSTM_WIDTH = MLSTM_HEADS * MLSTM_HEAD_DIM
MLSTM_BLOCK = CHUNK

EVEN_SPLITS = (SSD_D_INNER, SSD_CONV_DIM, SSD_HEADS, BAND_WIDTH, BAND_WIDTH, BAND_WIDTH)
EVEN_IN = sum(EVEN_SPLITS)
EVEN_MIX = SSD_D_INNER + BAND_WIDTH
ODD_SPLITS = (FOX_WIDTH, FOX_WIDTH, FOX_WIDTH, FOX_HEADS, MLSTM_WIDTH, MLSTM_WIDTH, MLSTM_WIDTH, MLSTM_HEADS, MLSTM_HEADS, MLSTM_WIDTH)
ODD_IN = sum(ODD_SPLITS)
ODD_MIX = FOX_WIDTH + MLSTM_WIDTH

F32 = jnp.float32
NEG_INF = -1e30

kernel_name = 'hybrid_streaming_encoder_step'


def rmsnorm(x, g):
    xf = x.astype(F32)
    y = xf * lax.rsqrt(jnp.mean(xf * xf, axis=-1, keepdims=True) + EPS)
    return (y * g.astype(F32)).astype(x.dtype)


def swiglu(x, wg, wu, wd):
    return (jax.nn.silu(x @ wg) * (x @ wu)) @ wd


def split_cols(x, sizes):
    cuts = [int(c) for c in np.cumsum(sizes)[:-1]]
    return jnp.split(x, cuts, axis=-1)


def to_blocks(a, blk):
    nb = a.shape[1] // blk
    a = a.reshape((a.shape[0], nb, blk) + a.shape[2:])
    return jnp.moveaxis(a, 1, 0)


def from_blocks(a):
    a = jnp.moveaxis(a, 0, 1)
    return a.reshape((a.shape[0], a.shape[1] * a.shape[2]) + a.shape[3:])


def causal_conv(x, buf, w, b):
    n = x.shape[1]
    xp = jnp.concatenate([buf.astype(x.dtype), x], axis=1)
    y = b
    for j in range(SSD_CONV):
        y = y + w[j] * xp[:, j:j + n]
    return jax.nn.silu(y), xp[:, n:]


def ssd_scan(xs, dt, a_log, bm, cm, s0):
    bsz, n, nh, hp = xs.shape
    ng, ns = bm.shape[2], bm.shape[3]
    nr = nh // ng
    blk = min(SSD_BLOCK, n)
    a = (dt * -jnp.exp(a_log.astype(F32))).reshape(bsz, n, ng, nr)
    xdt = (xs * dt[..., None]).reshape(bsz, n, ng, nr, hp)
    tri = jnp.tril(jnp.ones((blk, blk), bool))[None, :, :, None, None]

    def step(state, inp):
        a_b, x_b, b_b, c_b = inp
        acum = jnp.cumsum(a_b, axis=1)
        seg = acum[:, :, None] - acum[:, None, :]
        decay = jnp.where(tri, jnp.exp(jnp.where(tri, seg, 0.0)), 0.0)
        cb = jnp.einsum('btgn,bsgn->btsg', c_b, b_b)
        y = jnp.einsum('btsg,btsgr,bsgrp->btgrp', cb, decay, x_b)
        y = y + jnp.einsum('btgn,bgrpn->btgrp', c_b, state) * jnp.exp(acum)[..., None]
        tail = jnp.exp(acum[:, -1:] - acum)
        state = state * jnp.exp(acum[:, -1])[..., None, None] + jnp.einsum('bsgn,bsgr,bsgrp->bgrpn', b_b, tail, x_b)
        return state, y

    state, ys = lax.scan(step, s0.reshape(bsz, ng, nr, hp, ns),
                         (to_blocks(a, blk), to_blocks(xdt, blk), to_blocks(bm, blk), to_blocks(cm, blk)))
    return from_blocks(ys).reshape(bsz, n, nh, hp), state.reshape(bsz, nh, hp, ns)


def rel_bias(table, dist):
    return table[:, jnp.clip(dist, -REL_CLIP, REL_CLIP) + REL_CLIP]


def band_attn_prompt(q, k, v, rel_table):
    bsz, n, nh, hd = q.shape
    nc = n // CHUNK
    padw = ((0, 0), (BAND_PAST, 0), (0, 0), (0, 0))
    kc = jnp.pad(k, padw).reshape(bsz, nc + BAND_CHUNKS, CHUNK, nh, hd)
    vc = jnp.pad(v, padw).reshape(bsz, nc + BAND_CHUNKS, CHUNK, nh, hd)
    kb = jnp.stack([kc[:, w:w + nc] for w in range(BAND_CHUNKS + 1)], axis=2).reshape(bsz, nc, BAND_LEN, nh, hd)
    vb = jnp.stack([vc[:, w:w + nc] for w in range(BAND_CHUNKS + 1)], axis=2).reshape(bsz, nc, BAND_LEN, nh, hd)
    qi = jnp.arange(CHUNK)
    kj = jnp.arange(BAND_LEN)
    bias = rel_bias(rel_table, qi[:, None] + BAND_PAST - kj[None, :]).astype(F32)
    valid = (jnp.arange(nc)[:, None] * CHUNK - BAND_PAST + kj[None, :]) >= 0
    s = jnp.einsum('bcqhd,bckhd->bchqk', q.reshape(bsz, nc, CHUNK, nh, hd), kb,
                   preferred_element_type=F32) * (hd ** -0.5) + bias
    s = jnp.where(valid[None, :, None, None, :], s, NEG_INF)
    p = jax.nn.softmax(s, axis=-1)
    o = jnp.einsum('bchqk,bckhd->bcqhd', p.astype(v.dtype), vb)
    return o.reshape(bsz, n, nh * hd)


def band_attn_sample(q, k, v, k_cache, v_cache, rel_table):
    bsz, n, nh, hd = q.shape
    nb = k_cache.shape[1]
    kk = jnp.concatenate([k_cache.astype(k.dtype), k], axis=1)
    vv = jnp.concatenate([v_cache.astype(v.dtype), v], axis=1)
    kpos = jnp.concatenate([jnp.arange(nb) - nb, jnp.arange(n)])
    bias = rel_bias(rel_table, jnp.arange(n)[:, None] - kpos[None, :]).astype(F32)
    s = jnp.einsum('bqhd,bkhd->bhqk', q, kk, preferred_element_type=F32) * (hd ** -0.5) + bias
    p = jax.nn.softmax(s, axis=-1)
    o = jnp.einsum('bhqk,bkhd->bqhd', p.astype(v.dtype), vv)
    return o.reshape(bsz, n, nh * hd)


def fox_prompt(q, k, v, logf):
    bsz, n, nh, hd = q.shape
    ct = jnp.cumsum(logf, axis=1).transpose(0, 2, 1)
    kpos = jnp.arange(n)

    def block(i):
        start = i * Q_BLOCK
        qb = lax.dynamic_slice_in_dim(q, start, Q_BLOCK, axis=1)
        cq = lax.dynamic_slice_in_dim(ct, start, Q_BLOCK, axis=2)
        s = jnp.einsum('bqhd,bkhd->bhqk', qb, k, preferred_element_type=F32) * (hd ** -0.5)
        s = s + (cq[..., :, None] - ct[:, :, None, :])
        qpos = start + jnp.arange(Q_BLOCK)
        s = jnp.where(kpos[None, :] <= qpos[:, None], s, NEG_INF)
        p = jax.nn.softmax(s, axis=-1)
        return jnp.einsum('bhqk,bkhd->bqhd', p.astype(v.dtype), v)

    o = lax.map(block, jnp.arange(n // Q_BLOCK))
    return from_blocks(o).reshape(bsz, n, nh * hd)


def fox_sample(q, k, v, logf, k_cache, v_cache, logf_cache):
    bsz, n, nh, hd = q.shape
    npast = k_cache.shape[1]
    kk = jnp.concatenate([k_cache.astype(k.dtype), k], axis=1)
    vv = jnp.concatenate([v_cache.astype(v.dtype), v], axis=1)
    ct = jnp.cumsum(jnp.concatenate([logf_cache.astype(F32), logf], axis=1), axis=1).transpose(0, 2, 1)
    s = jnp.einsum('bqhd,bkhd->bhqk', q, kk, preferred_element_type=F32) * (hd ** -0.5)
    s = s + (ct[:, :, npast:, None] - ct[:, :, None, :])
    kpos = jnp.arange(npast + n)
    qpos = npast + jnp.arange(n)
    s = jnp.where(kpos[None, :] <= qpos[:, None], s, NEG_INF)
    p = jax.nn.softmax(s, axis=-1)
    o = jnp.einsum('bhqk,bkhd->bqhd', p.astype(v.dtype), vv)
    return o.reshape(bsz, n, nh * hd)


def mlstm_scan(q, k, v, ig, lf, c0, n0, m0):
    n = q.shape[1]
    blk = min(MLSTM_BLOCK, n)
    tri = jnp.tril(jnp.ones((blk, blk), bool))

    def step(carry, inp):
        cmat, nvec, m = carry
        qb, kb, vb, ib, fb = inp
        bcum = jnp.cumsum(fb, axis=1).transpose(0, 2, 1)
        ibt = ib.transpose(0, 2, 1)
        dmat = jnp.where(tri, bcum[..., :, None] - bcum[..., None, :] + ibt[..., None, :], NEG_INF)
        g = bcum + m[..., None]
        m_t = jnp.maximum(g, jnp.max(dmat, axis=-1))
        w = jnp.exp(dmat - m_t[..., None])
        inter = jnp.exp(g - m_t)
        a = w * jnp.einsum('bthd,bshd->bhts', qb, kb)
        num = inter[..., None] * jnp.einsum('bthd,bhde->bhte', qb, cmat) + jnp.einsum('bhts,bshe->bhte', a, vb)
        den = inter * jnp.einsum('bthd,bhd->bht', qb, nvec) + jnp.sum(a, axis=-1)
        h = num / jnp.maximum(jnp.abs(den), jnp.exp(-m_t))[..., None]
        w_end = w[:, :, -1]
        cmat = inter[..., -1, None, None] * cmat + jnp.einsum('bhs,bshd,bshe->bhde', w_end, kb, vb)
        nvec = inter[..., -1, None] * nvec + jnp.einsum('bhs,bshd->bhd', w_end, kb)
        return (cmat, nvec, m_t[..., -1]), h.transpose(0, 2, 1, 3)

    carry, hs = lax.scan(step, (c0, n0, m0),
                         (to_blocks(q, blk), to_blocks(k, blk), to_blocks(v, blk), to_blocks(ig, blk), to_blocks(lf, blk)))
    return from_blocks(hs), carry


def even_mix(u, w_in, w_out, conv_w, conv_b, a_log, dt_bias, d_skip, ssd_norm, q_norm, k_norm, rel_table, cache):
    bsz, n, _ = u.shape
    z, xbc, dt_raw, qr, kr, vr = split_cols(u @ w_in, EVEN_SPLITS)
    if cache is None:
        conv_buf = jnp.zeros((bsz, SSD_CONV - 1, SSD_CONV_DIM), u.dtype)
        s0 = jnp.zeros((bsz, SSD_HEADS, SSD_HEAD_DIM, SSD_STATE), F32)
    else:
        conv_buf, s0, k_cache, v_cache = cache
    xbc, new_buf = causal_conv(xbc, conv_buf, conv_w, conv_b)
    xs, bm, cm = split_cols(xbc, (SSD_D_INNER, SSD_GROUPS * SSD_STATE, SSD_GROUPS * SSD_STATE))
    xs = xs.reshape(bsz, n, SSD_HEADS, SSD_HEAD_DIM).astype(F32)
    dt = jax.nn.softplus(dt_raw.astype(F32) + dt_bias.astype(F32))
    y, s_new = ssd_scan(xs, dt, a_log,
                        bm.reshape(bsz, n, SSD_GROUPS, SSD_STATE).astype(F32),
                        cm.reshape(bsz, n, SSD_GROUPS, SSD_STATE).astype(F32), s0.astype(F32))
    y = (y + d_skip.astype(F32)[:, None] * xs).reshape(bsz, n, SSD_D_INNER)
    y_a = rmsnorm(y * jax.nn.silu(z.astype(F32)), ssd_norm).astype(u.dtype)
    q = rmsnorm(qr.reshape(bsz, n, BAND_HEADS, HEAD_DIM), q_norm)
    k = rmsnorm(kr.reshape(bsz, n, BAND_HEADS, HEAD_DIM), k_norm)
    v = vr.reshape(bsz, n, BAND_HEADS, HEAD_DIM)
    if cache is None:
        o_b = band_attn_prompt(q, k, v, rel_table)
        keep = min(BAND_PAST, n)
        k_new, v_new = k[:, n - keep:], v[:, n - keep:]
    else:
        o_b = band_attn_sample(q, k, v, k_cache, v_cache, rel_table)
        k_new, v_new = k, v
    out = jnp.concatenate([y_a, o_b.astype(u.dtype)], axis=-1) @ w_out
    return out, (new_buf, s_new.astype(u.dtype), k_new, v_new)


def odd_mix(u, w_in, w_out, q_norm, k_norm, f_bias, i_bias, mf_bias, ml_norm, cache):
    bsz, n, _ = u.shape
    fq, fk, fv, ff, mq, mk, mv, mi, mf, mo = split_cols(u @ w_in, ODD_SPLITS)
    q = rmsnorm(fq.reshape(bsz, n, FOX_HEADS, HEAD_DIM), q_norm)
    k = rmsnorm(fk.reshape(bsz, n, FOX_HEADS, HEAD_DIM), k_norm)
    v = fv.reshape(bsz, n, FOX_HEADS, HEAD_DIM)
    logf = jax.nn.log_sigmoid(ff.astype(F32) + f_bias.astype(F32))
    if cache is None:
        o_c = fox_prompt(q, k, v, logf)
        c0 = jnp.zeros((bsz, MLSTM_HEADS, MLSTM_HEAD_DIM, MLSTM_HEAD_DIM), F32)
        n0 = jnp.zeros((bsz, MLSTM_HEADS, MLSTM_HEAD_DIM), F32)
        m0 = jnp.zeros((bsz, MLSTM_HEADS), F32)
    else:
        k_cache, v_cache, lf_cache, c0, n0, m0 = cache
        o_c = fox_sample(q, k, v, logf, k_cache, v_cache, lf_cache)
    mq = mq.reshape(bsz, n, MLSTM_HEADS, MLSTM_HEAD_DIM).astype(F32)
    mk = mk.reshape(bsz, n, MLSTM_HEADS, MLSTM_HEAD_DIM).astype(F32) * (MLSTM_HEAD_DIM ** -0.5)
    mv = mv.reshape(bsz, n, MLSTM_HEADS, MLSTM_HEAD_DIM).astype(F32)
    ig = mi.astype(F32) + i_bias.astype(F32)
    lf = jax.nn.log_sigmoid(mf.astype(F32) + mf_bias.astype(F32))
    h, (c_new, n_new, m_new) = mlstm_scan(mq, mk, mv, ig, lf, c0.astype(F32), n0.astype(F32), m0.astype(F32))
    h = rmsnorm(h, ml_norm.reshape(MLSTM_HEADS, MLSTM_HEAD_DIM)).reshape(bsz, n, MLSTM_WIDTH)
    h = h * jax.nn.sigmoid(mo.astype(F32))
    out = jnp.concatenate([o_c.astype(u.dtype), h.astype(u.dtype)], axis=-1) @ w_out
    dtp = u.dtype
    return out, (k, v, logf.astype(dtp), c_new.astype(dtp), n_new.astype(dtp), m_new.astype(dtp))


def setup_inputs(seed: int = 0) -> dict:
    key = jax.random.key(seed)
    ks = iter(jax.random.split(key, 64))

    def nrm(shape, scale=1.0):
        return scale * jax.random.normal(next(ks), shape, F32)

    def lin(shape):
        return nrm(shape, shape[-2] ** -0.5)

    def gain(shape):
        return 1.0 + nrm(shape, 0.02)

    band_cache = min(BAND_PAST, PAST_LEN)
    dt0 = jnp.exp(jax.random.uniform(next(ks), (N_EVEN, SSD_HEADS), F32, math.log(1e-3), math.log(1e-1)))
    a0 = jax.random.uniform(next(ks), (N_EVEN, SSD_HEADS), F32, 1.0, 16.0)
    return {
        'x_prompt': nrm((BATCH, SEQ, D_MODEL)),
        'x_sample': nrm((DEC_BATCH, DEC_SEQ, D_MODEL)),
        'p_prompt': nrm((DEPTH, BATCH, SEQ, PLE_DIM)),
        'p_sample': nrm((DEPTH, DEC_BATCH, DEC_SEQ, PLE_DIM)),
        'state_ssd_conv': nrm((N_EVEN, DEC_BATCH, SSD_CONV - 1, SSD_CONV_DIM)),
        'state_ssd': nrm((N_EVEN, DEC_BATCH, SSD_HEADS, SSD_HEAD_DIM, SSD_STATE), 0.1),
        'cache_band_k': nrm((N_EVEN, DEC_BATCH, band_cache, BAND_HEADS, HEAD_DIM)),
        'cache_band_v': nrm((N_EVEN, DEC_BATCH, band_cache, BAND_HEADS, HEAD_DIM)),
        'cache_fox_k': nrm((N_ODD, DEC_BATCH, PAST_LEN, FOX_HEADS, HEAD_DIM)),
        'cache_fox_v': nrm((N_ODD, DEC_BATCH, PAST_LEN, FOX_HEADS, HEAD_DIM)),
        'cache_fox_logf': jax.nn.log_sigmoid(3.0 + nrm((N_ODD, DEC_BATCH, PAST_LEN, FOX_HEADS), 0.5)),
        'state_mlstm_C': nrm((N_ODD, DEC_BATCH, MLSTM_HEADS, MLSTM_HEAD_DIM, MLSTM_HEAD_DIM), 0.1),
        'state_mlstm_n': nrm((N_ODD, DEC_BATCH, MLSTM_HEADS, MLSTM_HEAD_DIM), 0.1),
        'state_mlstm_m': nrm((N_ODD, DEC_BATCH, MLSTM_HEADS), 0.5),
        'norm_ffn1': gain((DEPTH, D_MODEL)),
        'ffn1_wg': lin((DEPTH, D_MODEL, D_FF)),
        'ffn1_wu': lin((DEPTH, D_MODEL, D_FF)),
        'ffn1_wd': lin((DEPTH, D_FF, D_MODEL)),
        'norm_mix': gain((DEPTH, D_MODEL)),
        'norm_ffn2': gain((DEPTH, D_MODEL)),
        'ffn2_wg': lin((DEPTH, D_MODEL, D_FF)),
        'ffn2_wu': lin((DEPTH, D_MODEL, D_FF)),
        'ffn2_wd': lin((DEPTH, D_FF, D_MODEL)),
        'ple_proj': lin((DEPTH, PLE_DIM, D_MODEL)),
        'ple_norm': gain((DEPTH, D_MODEL)),
        'ple_gate_norm': gain((DEPTH, D_MODEL)),
        'ple_gate_w': lin((DEPTH, D_MODEL, D_MODEL)),
        'even_w_in': lin((N_EVEN, D_MODEL, EVEN_IN)),
        'even_w_out': lin((N_EVEN, EVEN_MIX, D_MODEL)),
        'ssd_conv_w': nrm((N_EVEN, SSD_CONV, SSD_CONV_DIM), 0.5),
        'ssd_conv_b': nrm((N_EVEN, SSD_CONV_DIM), 0.02),
        'ssd_a_log': jnp.log(a0),
        'ssd_dt_bias': dt0 + jnp.log(-jnp.expm1(-dt0)),
        'ssd_d': gain((N_EVEN, SSD_HEADS)),
        'ssd_norm': gain((N_EVEN, SSD_D_INNER)),
        'band_q_norm': gain((N_EVEN, HEAD_DIM)),
        'band_k_norm': gain((N_EVEN, HEAD_DIM)),
        'band_rel_bias': nrm((N_EVEN, BAND_HEADS, 2 * REL_CLIP + 1), 0.1),
        'odd_w_in': lin((N_ODD, D_MODEL, ODD_IN)),
        'odd_w_out': lin((N_ODD, ODD_MIX, D_MODEL)),
        'fox_q_norm': gain((N_ODD, HEAD_DIM)),
        'fox_k_norm': gain((N_ODD, HEAD_DIM)),
        'fox_f_bias': 3.0 + nrm((N_ODD, FOX_HEADS), 0.5),
        'mlstm_i_bias': nrm((N_ODD, MLSTM_HEADS), 0.1),
        'mlstm_f_bias': 3.0 + nrm((N_ODD, MLSTM_HEADS), 0.5),
        'mlstm_norm': gain((N_ODD, MLSTM_WIDTH)),
    }


def reference(x_prompt, x_sample, p_prompt, p_sample,
              state_ssd_conv, state_ssd, cache_band_k, cache_band_v,
              cache_fox_k, cache_fox_v, cache_fox_logf,
              state_mlstm_C, state_mlstm_n, state_mlstm_m,
              norm_ffn1, ffn1_wg, ffn1_wu, ffn1_wd,
              norm_mix, norm_ffn2, ffn2_wg, ffn2_wu, ffn2_wd,
              ple_proj, ple_norm, ple_gate_norm, ple_gate_w,
              even_w_in, even_w_out, ssd_conv_w, ssd_conv_b, ssd_a_log, ssd_dt_bias, ssd_d, ssd_norm,
              band_q_norm, band_k_norm, band_rel_bias,
              odd_w_in, odd_w_out, fox_q_norm, fox_k_norm, fox_f_bias,
              mlstm_i_bias, mlstm_f_bias, mlstm_norm):

    def run_layer(h, p, i, cache):
        j = i // 2
        h = h + 0.5 * swiglu(rmsnorm(h, norm_ffn1[i]), ffn1_wg[i], ffn1_wu[i], ffn1_wd[i])
        u = rmsnorm(h, norm_mix[i])
        if i % 2 == 0:
            mix, st = even_mix(u, even_w_in[j], even_w_out[j], ssd_conv_w[j], ssd_conv_b[j], ssd_a_log[j],
                               ssd_dt_bias[j], ssd_d[j], ssd_norm[j], band_q_norm[j], band_k_norm[j],
                               band_rel_bias[j], cache)
        else:
            mix, st = odd_mix(u, odd_w_in[j], odd_w_out[j], fox_q_norm[j], fox_k_norm[j], fox_f_bias[j],
                              mlstm_i_bias[j], mlstm_f_bias[j], mlstm_norm[j], cache)
        h = h + mix
        h = h + 0.5 * swiglu(rmsnorm(h, norm_ffn2[i]), ffn2_wg[i], ffn2_wu[i], ffn2_wd[i])
        e = rmsnorm(p @ ple_proj[i], ple_norm[i])
        h = h + e * jax.nn.sigmoid(rmsnorm(h, ple_gate_norm[i]) @ ple_gate_w[i])
        return h, st

    hp, hs = x_prompt, x_sample
    even_p, even_s, odd_p, odd_s = [], [], [], []
    for i in range(DEPTH):
        j = i // 2
        if i % 2 == 0:
            cache = (state_ssd_conv[j], state_ssd[j], cache_band_k[j], cache_band_v[j])
        else:
            cache = (cache_fox_k[j], cache_fox_v[j], cache_fox_logf[j],
                     state_mlstm_C[j], state_mlstm_n[j], state_mlstm_m[j])
        hp, st_p = run_layer(hp, p_prompt[i], i, None)
        hs, st_s = run_layer(hs, p_sample[i], i, cache)
        if i % 2 == 0:
            even_p.append(st_p)
            even_s.append(st_s)
        else:
            odd_p.append(st_p)
            odd_s.append(st_s)

    def stk(states, idx):
        return jnp.stack([s[idx] for s in states])

    return (hp, hs,
            stk(even_p, 0), stk(even_s, 0), stk(even_p, 1), stk(even_s, 1),
            stk(even_p, 2), stk(even_s, 2), stk(even_p, 3), stk(even_s, 3),
            stk(odd_p, 0), stk(odd_s, 0), stk(odd_p, 1), stk(odd_s, 1), stk(odd_p, 2), stk(odd_s, 2),
            stk(odd_p, 3), stk(odd_s, 3), stk(odd_p, 4), stk(odd_s, 4), stk(odd_p, 5), stk(odd_s, 5))
```

```python
import functools

import numpy as np
import jax
import jax.numpy as jnp
from jax import lax
from jax.experimental import pallas as pl
from jax.experimental.pallas import tpu as pltpu

F32 = jnp.float32
BF16 = jnp.bfloat16
EPS = 1e-6
NEG_INF = -1e30

D_MODEL = 1024
D_FF = 2816
PLE_DIM = 256
HEAD_DIM = 64
CHUNK = 64
SSD_HEADS = 16
SSD_D_INNER = 1024
SSD_CONV_DIM = 1280
SSD_CONV = 4
BAND_HEADS = 8
BAND_PAST = 512
REL_CLIP = 128
FOX_HEADS = 8
MLSTM_HEADS = 4
MLSTM_HEAD_DIM = 128
ATT_WIDTH = 512
GATE_LANES = 128
EVEN_SPLITS = (1024, 1280, 16, 512, 512, 512)
ODD_SPLITS = (512, 512, 512, 8, 512, 512, 512, 4, 4, 512)

V7X_VMEM_BYTES = 64 * 1024 * 1024
VMEM_LIMIT = 56 * 1024 * 1024
FF_CHUNK = 256


def _cparams(n_axes):
    return pltpu.CompilerParams(dimension_semantics=("arbitrary",) * n_axes,
                                vmem_limit_bytes=VMEM_LIMIT)


def _resident(shape):
    nd = len(shape)
    return pl.BlockSpec(shape, lambda *_: (0,) * nd, pipeline_mode=pl.Buffered(1))


def _dot(a, b):
    return jnp.dot(a, b, preferred_element_type=F32)


def _dot_nt(a, b):
    return lax.dot_general(a, b, (((1,), (1,)), ((), ())), preferred_element_type=F32)


def _dot_tn(a, b):
    return lax.dot_general(a, b, (((0,), (0,)), ((), ())), preferred_element_type=F32)


def _split3(x):
    hi = x.astype(BF16)
    r = x - hi.astype(F32)
    mid = r.astype(BF16)
    lo = (r - mid.astype(F32)).astype(BF16)
    return hi, mid, lo


def _dot3_l(x, w01):
    hi, mid, lo = _split3(x)
    return _dot(hi, w01) + _dot(mid, w01) + _dot(lo, w01)


def _dot3_r(w01, x):
    hi, mid, lo = _split3(x)
    return _dot(w01, hi) + _dot(w01, mid) + _dot(w01, lo)


def _rms(x, g):
    ms = jnp.mean(x * x, axis=-1, keepdims=True)
    return x * lax.rsqrt(ms + EPS) * g


def _sigmoid(x):
    return 1.0 / (1.0 + jnp.exp(-x))


def _silu(x):
    return x * _sigmoid(x)


def _softplus(x):
    return jnp.maximum(x, 0.0) + jnp.log(1.0 + jnp.exp(-jnp.abs(x)))


def _log_sigmoid(x):
    return jnp.minimum(x, 0.0) - jnp.log(1.0 + jnp.exp(-jnp.abs(x)))


def _lower_tri(n):
    r = lax.broadcasted_iota(jnp.int32, (n, n), 0)
    c = lax.broadcasted_iota(jnp.int32, (n, n), 1)
    return r >= c


def _ffn_half(x, g_ref, wg_ref, wu_ref, wd_ref, acc_ref):
    xn = _rms(x, g_ref[...]).astype(BF16)
    for c in range(D_FF // FF_CHUNK):
        sl = slice(c * FF_CHUNK, (c + 1) * FF_CHUNK)
        gate = _dot(xn, wg_ref[:, sl])
        up = _dot(xn, wu_ref[:, sl])
        a = (_silu(gate) * up).astype(BF16)
        d = _dot(a, wd_ref[sl, :])
        if c == 0:
            acc_ref[...] = d
        else:
            acc_ref[...] += d
    return 0.5 * acc_ref[...]


def _ffn1_kernel(h_ref, g_ref, wg_ref, wu_ref, wd_ref, o_ref, acc_ref):
    x = h_ref[...]
    o_ref[...] = x + _ffn_half(x, g_ref, wg_ref, wu_ref, wd_ref, acc_ref)


def _ffn1(h, g, wg, wu, wd, tm):
    n = h.shape[0]
    tok = pl.BlockSpec((tm, D_MODEL), lambda i: (i, 0))
    return pl.pallas_call(
        _ffn1_kernel,
        grid=(n // tm,),
        in_specs=[tok, _resident((1, D_MODEL)), _resident((D_MODEL, D_FF)),
                  _resident((D_MODEL, D_FF)), _resident((D_FF, D_MODEL))],
        out_specs=tok,
        out_shape=jax.ShapeDtypeStruct((n, D_MODEL), F32),
        scratch_shapes=[pltpu.VMEM((tm, D_MODEL), F32)],
        compiler_params=_cparams(1),
        name="ffn1",
    )(h, g, wg, wu, wd)


def _head_rms(x, bd_ref, g, width):
    x2 = x * x
    hi = x2.astype(BF16)
    lo = (x2 - hi.astype(F32)).astype(BF16)
    ms = (_dot(hi, bd_ref[...]) + _dot(lo, bd_ref[...])) * (1.0 / width)
    return x * lax.rsqrt(ms + EPS) * g


def _proj_even_kernel(h_ref, gm_ref, wz_ref, wx_ref, wdt_ref, wq_ref, wk_ref, wv_ref,
                      dtb_ref, gq_ref, gk_ref, bd_ref,
                      z_ref, xbc_ref, dt_ref, q_ref, k_ref, v_ref):
    u = _rms(h_ref[...], gm_ref[...]).astype(BF16)
    z_ref[...] = _dot(u, wz_ref[...])
    xbc_ref[...] = _dot(u, wx_ref[...])
    dt_ref[...] = _softplus(_dot(u, wdt_ref[...]) + dtb_ref[...])
    q_ref[...] = _head_rms(_dot(u, wq_ref[...]), bd_ref, gq_ref[...], HEAD_DIM)
    k_ref[...] = _head_rms(_dot(u, wk_ref[...]), bd_ref, gk_ref[...], HEAD_DIM)
    v_ref[...] = _dot(u, wv_ref[...])


def _proj_even(h, gm, wz, wx, wdt, wq, wk, wv, dtb, gq, gk, bd, tm):
    n = h.shape[0]

    def tok(c):
        return pl.BlockSpec((tm, c), lambda i: (i, 0))

    widths = (D_MODEL, SSD_CONV_DIM, GATE_LANES, ATT_WIDTH, ATT_WIDTH, ATT_WIDTH)
    return pl.pallas_call(
        _proj_even_kernel,
        grid=(n // tm,),
        in_specs=[tok(D_MODEL), _resident((1, D_MODEL)),
                  _resident(wz.shape), _resident(wx.shape), _resident(wdt.shape),
                  _resident(wq.shape), _resident(wk.shape), _resident(wv.shape),
                  _resident(dtb.shape), _resident(gq.shape), _resident(gk.shape), _resident(bd.shape)],
        out_specs=[tok(c) for c in widths],
        out_shape=[jax.ShapeDtypeStruct((n, c), F32) for c in widths],
        compiler_params=_cparams(1),
        name="proj_even",
    )(h, gm, wz, wx, wdt, wq, wk, wv, dtb, gq, gk, bd)


def _proj_odd_kernel(h_ref, gm_ref, wfq_ref, wfk_ref, wfv_ref, wgt_ref, wmq_ref, wmk_ref, wmv_ref, wmo_ref,
                     gtb_ref, gq_ref, gk_ref, bd_ref,
                     fq_ref, fk_ref, fv_ref, gt_ref, mq_ref, mk_ref, mv_ref, so_ref):
    u = _rms(h_ref[...], gm_ref[...]).astype(BF16)
    fq_ref[...] = _head_rms(_dot(u, wfq_ref[...]), bd_ref, gq_ref[...], HEAD_DIM)
    fk_ref[...] = _head_rms(_dot(u, wfk_ref[...]), bd_ref, gk_ref[...], HEAD_DIM)
    fv_ref[...] = _dot(u, wfv_ref[...])
    pre = _dot(u, wgt_ref[...]) + gtb_ref[...]
    lane = lax.broadcasted_iota(jnp.int32, pre.shape, 1)
    is_input_gate = (lane >= FOX_HEADS) & (lane < FOX_HEADS + MLSTM_HEADS)
    gt_ref[...] = jnp.where(is_input_gate, pre, _log_sigmoid(pre))
    mq_ref[...] = _dot(u, wmq_ref[...])
    mk_ref[...] = _dot(u, wmk_ref[...]) * (MLSTM_HEAD_DIM ** -0.5)
    mv_ref[...] = _dot(u, wmv_ref[...])
    so_ref[...] = _sigmoid(_dot(u, wmo_ref[...]))


def _proj_odd(h, gm, wfq, wfk, wfv, wgt, wmq, wmk, wmv, wmo, gtb, gq, gk, bd, tm):
    n = h.shape[0]

    def tok(c):
        return pl.BlockSpec((tm, c), lambda i: (i, 0))

    widths = (ATT_WIDTH, ATT_WIDTH, ATT_WIDTH, GATE_LANES, ATT_WIDTH, ATT_WIDTH, ATT_WIDTH, ATT_WIDTH)
    ws = (wfq, wfk, wfv, wgt, wmq, wmk, wmv, wmo, gtb, gq, gk, bd)
    return pl.pallas_call(
        _proj_odd_kernel,
        grid=(n // tm,),
        in_specs=[tok(D_MODEL), _resident((1, D_MODEL))] + [_resident(w.shape) for w in ws],
        out_specs=[tok(c) for c in widths],
        out_shape=[jax.ShapeDtypeStruct((n, c), F32) for c in widths],
        compiler_params=_cparams(1),
        name="proj_odd",
    )(h, gm, *ws)


def _tail_kernel(h_ref, a_ref, b_ref, p_ref, woa_ref, wob_ref, g2_ref, wg_ref, wu_ref, wd_ref,
                 wple_ref, gple_ref, ggate_ref, wgate_ref, o_ref, acc_ref):
    h = h_ref[...] + _dot(a_ref[...].astype(BF16), woa_ref[...]) + _dot(b_ref[...].astype(BF16), wob_ref[...])
    h = h + _ffn_half(h, g2_ref, wg_ref, wu_ref, wd_ref, acc_ref)
    e = _rms(_dot(p_ref[...].astype(BF16), wple_ref[...]), gple_ref[...])
    gate = _sigmoid(_dot(_rms(h, ggate_ref[...]).astype(BF16), wgate_ref[...]))
    o_ref[...] = h + e * gate


def _tail(h, a, b, p, woa, wob, g2, wg, wu, wd, wple, gple, ggate, wgate, tm):
    n = h.shape[0]

    def tok(c):
        return pl.BlockSpec((tm, c), lambda i: (i, 0))

    ws = (woa, wob, g2, wg, wu, wd, wple, gple, ggate, wgate)
    return pl.pallas_call(
        _tail_kernel,
        grid=(n // tm,),
        in_specs=[tok(D_MODEL), tok(a.shape[1]), tok(b.shape[1]), tok(PLE_DIM)] + [_resident(w.shape) for w in ws],
        out_specs=tok(D_MODEL),
        out_shape=jax.ShapeDtypeStruct((n, D_MODEL), F32),
        scratch_shapes=[pltpu.VMEM((tm, D_MODEL), F32)],
        compiler_params=_cparams(1),
        name="tail",
    )(h, a, b, p, *ws)


CONV_PAD = 8
SSD_PAIR = 128
SSD_GROUP_COLS = 512
SSD_STATE = 64


def _ssd_kernel(xbc_ref, z_ref, dt_ref, dtT_ref, conv0_ref, st0_ref, cw_ref, cb_ref,
                alog_ref, alogT_ref, dskip_ref, gn_ref, e_ref,
                y_ref, convo_ref, sto_ref, xp_ref, st_ref, *, tb, t):
    i = pl.program_id(1)

    @pl.when(i == 0)
    def _():
        xp_ref[0:CONV_PAD, :] = conv0_ref[...]
        st_ref[...] = jnp.zeros_like(st_ref)
        st_ref[0:SSD_STATE, 0:SSD_GROUP_COLS] = st0_ref[0]
        st_ref[SSD_STATE:2 * SSD_STATE, SSD_GROUP_COLS:2 * SSD_GROUP_COLS] = st0_ref[1]

    xp_ref[CONV_PAD:CONV_PAD + tb, :] = xbc_ref[...]
    cw = cw_ref[...]
    pre = cb_ref[...]
    for j in range(SSD_CONV):
        off = CONV_PAD - (SSD_CONV - 1) + j
        pre = pre + cw[j:j + 1, :] * xp_ref[off:off + tb, :]
    xc = _silu(pre)
    tail_rows = xp_ref[tb:tb + CONV_PAD, :]
    convo_ref[...] = tail_rows
    xp_ref[0:CONV_PAD, :] = tail_rows

    tri = _lower_tri(t)
    ltri = tri.astype(BF16)
    utri = (lax.broadcasted_iota(jnp.int32, (t, t), 0) <= lax.broadcasted_iota(jnp.int32, (t, t), 1)).astype(BF16)
    a_row = -jnp.exp(alog_ref[...])
    a_col = -jnp.exp(alogT_ref[...])
    e01 = e_ref[...]
    lane = lax.broadcasted_iota(jnp.int32, (t, SSD_PAIR), 1)
    st_r = lax.broadcasted_iota(jnp.int32, st_ref.shape, 0)
    st_c = lax.broadcasted_iota(jnp.int32, st_ref.shape, 1)
    st_own = (st_r < SSD_STATE) == (st_c < SSD_GROUP_COLS)

    for c in range(tb // t):
        r0 = c * t
        xs = xc[r0:r0 + t, 0:SSD_D_INNER]
        b2 = xc[r0:r0 + t, SSD_D_INNER:SSD_D_INNER + 128]
        c2 = xc[r0:r0 + t, SSD_D_INNER + 128:SSD_D_INNER + 256]
        dtc = dt_ref[r0:r0 + t, :]
        acum = _dot3_r(ltri, dtc * a_row)
        acum_t = _dot3_l(dtT_ref[:, r0:r0 + t] * a_col, utri)
        acum_x = _dot3_l(acum, e01)
        xdt = xs * _dot3_l(dtc, e01)
        last = acum_x[t - 1:t, :]
        tailx = jnp.exp(last - acum_x) * xdt
        b2b = b2.astype(BF16)
        y = _dot(c2.astype(BF16), st_ref[...].astype(BF16)) * jnp.exp(acum_x) + dskip_ref[...] * xs
        cbs = []
        for g in range(2):
            in_g = (lane >= HEAD_DIM * g) & (lane < HEAD_DIM * (g + 1))
            cbs.append(_dot_nt(jnp.where(in_g, c2, 0.0).astype(BF16), b2b))
        ys = []
        for j in range(SSD_HEADS // 2):
            xpair = xdt[:, SSD_PAIR * j:SSD_PAIR * (j + 1)]
            acc = None
            for s in range(2):
                hd = 2 * j + s
                seg = acum[:, hd:hd + 1] - acum_t[hd:hd + 1, :]
                m = jnp.exp(jnp.where(tri, seg, NEG_INF)) * cbs[j // 4]
                xh = jnp.where((lane < HEAD_DIM) if s == 0 else (lane >= HEAD_DIM), xpair, 0.0)
                d = _dot(m.astype(BF16), xh.astype(BF16))
                acc = d if acc is None else acc + d
            ys.append(acc)
        y = y + jnp.concatenate(ys, axis=1)
        upd = _dot_tn(b2b, tailx.astype(BF16))
        st_ref[...] = jnp.where(st_own, st_ref[...] * jnp.exp(last) + upd, 0.0)
        y_ref[r0:r0 + t, :] = _rms(y * _silu(z_ref[r0:r0 + t, :]), gn_ref[...])

    sto_ref[0] = st_ref[0:SSD_STATE, 0:SSD_GROUP_COLS]
    sto_ref[1] = st_ref[SSD_STATE:2 * SSD_STATE, SSD_GROUP_COLS:2 * SSD_GROUP_COLS]


def _ssd(xbc, z, dt, dt_t, conv0, st0, cw, cb, alog, alog_t, dskip, gn, e01, tb, t):
    bsz, n, _ = xbc.shape
    consts = (cw, cb, alog, alog_t, dskip, gn, e01)
    return pl.pallas_call(
        functools.partial(_ssd_kernel, tb=tb, t=t),
        grid=(bsz, n // tb),
        in_specs=[pl.BlockSpec((None, tb, SSD_CONV_DIM), lambda b, i: (b, i, 0)),
                  pl.BlockSpec((None, tb, SSD_D_INNER), lambda b, i: (b, i, 0)),
                  pl.BlockSpec((None, tb, GATE_LANES), lambda b, i: (b, i, 0)),
                  pl.BlockSpec((None, SSD_HEADS, tb), lambda b, i: (b, 0, i)),
                  pl.BlockSpec((None, CONV_PAD, SSD_CONV_DIM), lambda b, i: (b, 0, 0)),
                  pl.BlockSpec((None, 2, SSD_STATE, SSD_GROUP_COLS), lambda b, i: (b, 0, 0, 0))]
                 + [_resident(w.shape) for w in consts],
        out_specs=[pl.BlockSpec((None, tb, SSD_D_INNER), lambda b, i: (b, i, 0)),
                   pl.BlockSpec((None, CONV_PAD, SSD_CONV_DIM), lambda b, i: (b, 0, 0)),
                   pl.BlockSpec((None, 2, SSD_STATE, SSD_GROUP_COLS), lambda b, i: (b, 0, 0, 0))],
        out_shape=[jax.ShapeDtypeStruct((bsz, n, SSD_D_INNER), F32),
                   jax.ShapeDtypeStruct((bsz, CONV_PAD, SSD_CONV_DIM), F32),
                   jax.ShapeDtypeStruct((bsz, 2, SSD_STATE, SSD_GROUP_COLS), F32)],
        scratch_shapes=[pltpu.VMEM((CONV_PAD + tb, SSD_CONV_DIM), F32),
                        pltpu.VMEM((2 * SSD_STATE, 2 * SSD_GROUP_COLS), F32)],
        compiler_params=_cparams(2),
        name="ssd",
    )(xbc, z, dt, dt_t, conv0, st0, *consts)


def _head_expand(n_heads, width):
    e = np.zeros((GATE_LANES, n_heads * width), np.float32)
    for h in range(n_heads):
        e[h, h * width:(h + 1) * width] = 1.0
    return jnp.asarray(e, BF16)


def _ssd_mixer(xbc, z, dt, conv_buf, s0, conv_w, conv_b, a_log, d_skip, ssd_norm, tb, t):
    bsz = xbc.shape[0]
    dt_t = jnp.swapaxes(dt[..., :SSD_HEADS], 1, 2)
    conv0 = jnp.pad(conv_buf, ((0, 0), (CONV_PAD - (SSD_CONV - 1), 0), (0, 0)))
    st0 = s0.reshape(bsz, 2, 8, HEAD_DIM, SSD_STATE).transpose(0, 1, 4, 2, 3).reshape(bsz, 2, SSD_STATE, SSD_GROUP_COLS)
    cw = jnp.pad(conv_w, ((0, CONV_PAD - SSD_CONV), (0, 0)))
    alog = jnp.pad(a_log, (0, GATE_LANES - SSD_HEADS)).reshape(1, GATE_LANES)
    y, convo, sto = _ssd(xbc, z, dt, dt_t, conv0, st0, cw, conv_b.reshape(1, -1), alog, a_log.reshape(SSD_HEADS, 1),
                         jnp.repeat(d_skip, HEAD_DIM).reshape(1, -1), ssd_norm.reshape(1, -1),
                         _head_expand(SSD_HEADS, HEAD_DIM), tb, t)
    s_new = sto.reshape(bsz, 2, SSD_STATE, 8, HEAD_DIM).transpose(0, 1, 3, 4, 2).reshape(bsz, SSD_HEADS, HEAD_DIM, SSD_STATE)
    return y, convo[:, CONV_PAD - (SSD_CONV - 1):], s_new


GATE_I = FOX_HEADS
GATE_F = FOX_HEADS + MLSTM_HEADS


def _mlstm_kernel(q_ref, k_ref, v_ref, so_ref, gt_ref, gtT_ref, c0_ref, n0_ref, m0_ref, gn_ref,
                  h_ref, co_ref, no_ref, mo_ref, c_ref, n_ref, m_ref, *, t):
    i = pl.program_id(1)

    @pl.when(i == 0)
    def _():
        c_ref[...] = c0_ref[...]
        n_ref[...] = n0_ref[...]
        m_ref[...] = m0_ref[...]

    tri = _lower_tri(t)
    ltri = tri.astype(BF16)
    utri = (lax.broadcasted_iota(jnp.int32, (t, t), 0) <= lax.broadcasted_iota(jnp.int32, (t, t), 1)).astype(BF16)
    gt = gt_ref[...]
    gt_t = gtT_ref[...]
    cum = _dot3_r(ltri, gt)
    cum_t = _dot3_l(gt_t, utri)
    m_vec = m_ref[...]
    m_lane = lax.broadcasted_iota(jnp.int32, m_vec.shape, 1)
    m_next = m_vec
    for hh in range(MLSTM_HEADS):
        sl = slice(MLSTM_HEAD_DIM * hh, MLSTM_HEAD_DIM * (hh + 1))
        bcol = cum[:, GATE_F + hh:GATE_F + hh + 1]
        brow = cum_t[GATE_F + hh:GATE_F + hh + 1, :]
        ig_row = gt_t[GATE_I + hh:GATE_I + hh + 1, :]
        ig_col = gt[:, GATE_I + hh:GATE_I + hh + 1]
        dmat = jnp.where(tri, bcol - brow + ig_row, NEG_INF)
        g = bcol + m_vec[:, hh:hh + 1]
        m_t = jnp.maximum(g, jnp.max(dmat, axis=-1, keepdims=True))
        w = jnp.exp(dmat - m_t)
        inter = jnp.exp(g - m_t)
        qh = q_ref[:, sl]
        kh = k_ref[:, sl]
        vh = v_ref[:, sl]
        qb = qh.astype(BF16)
        kb = kh.astype(BF16)
        a = w * _dot_nt(qb, kb)
        cm = c_ref[hh]
        nv = n_ref[hh:hh + 1, :]
        num = inter * _dot(qb, cm.astype(BF16)) + _dot(a.astype(BF16), vh.astype(BF16))
        den = inter * jnp.sum(qh * nv, axis=-1, keepdims=True) + jnp.sum(a, axis=-1, keepdims=True)
        hout = num / jnp.maximum(jnp.abs(den), jnp.exp(-m_t))
        m_last = m_t[t - 1:t, :]
        inter_last = inter[t - 1:t, :]
        w_end = jnp.exp(bcol[t - 1:t, :] - bcol + ig_col - m_last)
        c_ref[hh] = inter_last * cm + _dot_tn(kb, (w_end * vh).astype(BF16))
        n_ref[hh:hh + 1, :] = inter_last * nv + jnp.sum(w_end * kh, axis=0, keepdims=True)
        m_next = jnp.where(m_lane == hh, m_last, m_next)
        h_ref[:, sl] = _rms(hout, gn_ref[:, sl]) * so_ref[:, sl]
    m_ref[...] = m_next
    co_ref[...] = c_ref[...]
    no_ref[...] = n_ref[...]
    mo_ref[...] = m_next


def _mlstm(q, k, v, so, gt, gt_t, c0, n0, m0, gn, t):
    bsz, n, _ = q.shape
    tok = pl.BlockSpec((None, t, ATT_WIDTH), lambda b, i: (b, i, 0))
    c_spec = pl.BlockSpec((None, MLSTM_HEADS, MLSTM_HEAD_DIM, MLSTM_HEAD_DIM), lambda b, i: (b, 0, 0, 0))
    n_spec = pl.BlockSpec((None, MLSTM_HEADS, MLSTM_HEAD_DIM), lambda b, i: (b, 0, 0))
    m_spec = pl.BlockSpec((None, 1, GATE_LANES), lambda b, i: (b, 0, 0))
    return pl.pallas_call(
        functools.partial(_mlstm_kernel, t=t),
        grid=(bsz, n // t),
        in_specs=[tok, tok, tok, tok,
                  pl.BlockSpec((None, t, GATE_LANES), lambda b, i: (b, i, 0)),
                  pl.BlockSpec((None, 16, t), lambda b, i: (b, 0, i)),
                  c_spec, n_spec, m_spec, _resident(gn.shape)],
        out_specs=[tok, c_spec, n_spec, m_spec],
        out_shape=[jax.ShapeDtypeStruct((bsz, n, ATT_WIDTH), F32),
                   jax.ShapeDtypeStruct(c0.shape, F32),
                   jax.ShapeDtypeStruct(n0.shape, F32),
                   jax.ShapeDtypeStruct(m0.shape, F32)],
        scratch_shapes=[pltpu.VMEM((MLSTM_HEADS, MLSTM_HEAD_DIM, MLSTM_HEAD_DIM), F32),
                        pltpu.VMEM((MLSTM_HEADS, MLSTM_HEAD_DIM), F32),
                        pltpu.VMEM((1, GATE_LANES), F32)],
        compiler_params=_cparams(2),
        name="mlstm",
    )(q, k, v, so, gt, gt_t, c0, n0, m0, gn)


def _mlstm_mixer(mq, mk, mv, so, gates, c0, n0, m0, ml_norm, t):
    gt_t = jnp.swapaxes(gates[..., :16], 1, 2)
    m0p = jnp.pad(m0, ((0, 0), (0, GATE_LANES - MLSTM_HEADS)))[:, None, :]
    h, c_new, n_new, m_new = _mlstm(mq, mk, mv, so, gates, gt_t, c0, n0, m0p, ml_norm.reshape(1, -1), t)
    return h, c_new, n_new, m_new[:, 0, :MLSTM_HEADS]


PAIR = 2 * HEAD_DIM
N_PAIRS = ATT_WIDTH // PAIR
QK_SCALE = HEAD_DIM ** -0.5


def _in_head(lane, h):
    return (lane >= HEAD_DIM * h) & (lane < HEAD_DIM * (h + 1))


def _stack_heads(q):
    lane = lax.broadcasted_iota(jnp.int32, q.shape, 1)
    return jnp.concatenate([jnp.where(_in_head(lane, h), q, 0.0) for h in range(ATT_WIDTH // HEAD_DIM)], axis=0)


def _unstack_heads(o, nq):
    lane = lax.broadcasted_iota(jnp.int32, (nq, ATT_WIDTH), 1)
    out = o[0:nq, :]
    for h in range(1, ATT_WIDTH // HEAD_DIM):
        out = jnp.where(_in_head(lane, h), o[h * nq:(h + 1) * nq, :], out)
    return out


def _rows_per_head(x_t, nq):
    n = x_t.shape[1]
    return jnp.concatenate([jnp.broadcast_to(x_t[h:h + 1, :], (nq, n)) for h in range(x_t.shape[0])], axis=0)


BAND_QB = BAND_PAST // 2


def _band_prompt_kernel(q_ref, k2_ref, k1_ref, k0_ref, v2_ref, v1_ref, v0_ref, bias_ref, o_ref):
    i = pl.program_id(2)
    qb = q_ref.shape[0]
    lane = lax.broadcasted_iota(jnp.int32, (qb, PAIR), 1)
    q = q_ref[...] * QK_SCALE
    kcat = jnp.concatenate([k2_ref[...], k1_ref[...], k0_ref[...]], axis=0).astype(BF16)
    vcat = jnp.concatenate([v2_ref[...], v1_ref[...], v0_ref[...]], axis=0).astype(BF16)
    col = lax.broadcasted_iota(jnp.int32, (qb, 3 * qb), 1)
    in_sequence = col >= (2 - i) * qb
    outs = []
    for s in range(2):
        qh = jnp.where(_in_head(lane, s), q, 0.0).astype(BF16)
        sc = jnp.where(in_sequence, _dot_nt(qh, kcat) + bias_ref[s], NEG_INF)
        p = jnp.exp(sc - jnp.max(sc, axis=-1, keepdims=True))
        outs.append(_dot(p.astype(BF16), vcat) / jnp.sum(p, axis=-1, keepdims=True))
    o_ref[...] = jnp.where(lane < HEAD_DIM, outs[0], outs[1])


def _band_prompt(q, k, v, bias):
    bsz, n, _ = q.shape
    qb = BAND_QB

    def kv_spec(back):
        return pl.BlockSpec((None, qb, PAIR), lambda b, j, i: (b, jnp.maximum(i - back, 0), j))

    return pl.pallas_call(
        _band_prompt_kernel,
        grid=(bsz, N_PAIRS, n // qb),
        in_specs=[kv_spec(0), kv_spec(2), kv_spec(1), kv_spec(0), kv_spec(2), kv_spec(1), kv_spec(0),
                  pl.BlockSpec((2, qb, 3 * qb), lambda b, j, i: (j, 0, 0))],
        out_specs=kv_spec(0),
        out_shape=jax.ShapeDtypeStruct((bsz, n, ATT_WIDTH), F32),
        compiler_params=_cparams(3),
        name="band_prompt",
    )(q, k, k, k, v, v, v, bias)


def _band_prompt_bias(rel_table):
    qb = BAND_QB
    r = np.arange(qb)[:, None] + 2 * qb
    c = np.arange(3 * qb)[None, :]
    in_band = (c // CHUNK <= r // CHUNK) & (c // CHUNK >= r // CHUNK - BAND_PAST // CHUNK)
    idx = np.clip(r - c, -REL_CLIP, REL_CLIP) + REL_CLIP
    return jnp.where(jnp.asarray(in_band)[None], rel_table[:, idx], NEG_INF).astype(F32)


def _band_sample_kernel(q_ref, kc_ref, vc_ref, kn_ref, vn_ref, bc_ref, bn_ref, o_ref):
    nq = q_ref.shape[0]
    qs = _stack_heads(q_ref[...] * QK_SCALE).astype(BF16)
    s_c = _dot_nt(qs, kc_ref[...].astype(BF16)) + bc_ref[...]
    s_n = _dot_nt(qs, kn_ref[...].astype(BF16)) + bn_ref[...]
    m = jnp.maximum(jnp.max(s_c, axis=-1, keepdims=True), jnp.max(s_n, axis=-1, keepdims=True))
    p_c = jnp.exp(s_c - m)
    p_n = jnp.exp(s_n - m)
    l = jnp.sum(p_c, axis=-1, keepdims=True) + jnp.sum(p_n, axis=-1, keepdims=True)
    o = _dot(p_c.astype(BF16), vc_ref[...].astype(BF16)) + _dot(p_n.astype(BF16), vn_ref[...].astype(BF16))
    o_ref[...] = _unstack_heads(o / l, nq)


def _band_sample(q, k, v, k_cache, v_cache, bias_c, bias_n):
    bsz, nq, _ = q.shape
    nb = k_cache.shape[1]
    new = pl.BlockSpec((None, nq, ATT_WIDTH), lambda b: (b, 0, 0))
    old = pl.BlockSpec((None, nb, ATT_WIDTH), lambda b: (b, 0, 0))
    return pl.pallas_call(
        _band_sample_kernel,
        grid=(bsz,),
        in_specs=[new, old, old, new, new, _resident(bias_c.shape), _resident(bias_n.shape)],
        out_specs=new,
        out_shape=jax.ShapeDtypeStruct((bsz, nq, ATT_WIDTH), F32),
        compiler_params=_cparams(1),
        name="band_sample",
    )(q, k_cache, v_cache, k, v, bias_c, bias_n)


def _band_sample_bias(rel_table, nq, nb):
    kpos = np.concatenate([np.arange(nb) - nb, np.arange(nq)])
    idx = np.clip(np.arange(nq)[:, None] - kpos[None, :], -REL_CLIP, REL_CLIP) + REL_CLIP
    bias = rel_table[:, idx].astype(F32).reshape(BAND_HEADS * nq, nb + nq)
    return bias[:, :nb], bias[:, nb:]


def _cumsum_lanes_kernel(x_ref, o_ref, carry_ref):
    lb = x_ref.shape[1]

    @pl.when(pl.program_id(0) == 0)
    def _():
        carry_ref[...] = jnp.zeros_like(carry_ref)

    utri = (lax.broadcasted_iota(jnp.int32, (lb, lb), 0) <= lax.broadcasted_iota(jnp.int32, (lb, lb), 1)).astype(BF16)
    c = _dot3_l(x_ref[...], utri) + carry_ref[...]
    o_ref[...] = c
    carry_ref[...] = c[:, lb - 1:lb]


def _cumsum_lanes(x, lb):
    rows, n = x.shape
    blk = pl.BlockSpec((rows, lb), lambda i: (0, i))
    return pl.pallas_call(
        _cumsum_lanes_kernel,
        grid=(n // lb,),
        in_specs=[blk],
        out_specs=blk,
        out_shape=jax.ShapeDtypeStruct((rows, n), F32),
        scratch_shapes=[pltpu.VMEM((rows, 1), F32)],
        compiler_params=_cparams(1),
        name="cumsum_lanes",
    )(x)


FOX_BLK = 512


def _fox_prompt_kernel(q_ref, k_ref, v_ref, cq_ref, ck_ref, o_ref, m_ref, l_ref, acc_ref):
    i = pl.program_id(2)
    blk = q_ref.shape[0]
    lane = lax.broadcasted_iota(jnp.int32, (blk, PAIR), 1)
    q = q_ref[...] * QK_SCALE
    qh = [jnp.where(_in_head(lane, s), q, 0.0).astype(BF16) for s in range(2)]
    cq = cq_ref[...]
    tri = _lower_tri(blk)
    m_ref[...] = jnp.full_like(m_ref, NEG_INF)
    l_ref[...] = jnp.zeros_like(l_ref)
    acc_ref[...] = jnp.zeros_like(acc_ref)

    def step(kk, on_diagonal):
        k0 = pl.multiple_of(kk * blk, blk)
        kb = k_ref[pl.ds(k0, blk), :].astype(BF16)
        vb = v_ref[pl.ds(k0, blk), :].astype(BF16)
        alphas, pvs = [], []
        for s in range(2):
            sc = _dot_nt(qh[s], kb) + (cq[:, s:s + 1] - ck_ref[s:s + 1, pl.ds(k0, blk)])
            if on_diagonal:
                sc = jnp.where(tri, sc, NEG_INF)
            m_old = m_ref[s]
            m_new = jnp.maximum(m_old, jnp.max(sc, axis=-1, keepdims=True))
            alpha = jnp.exp(m_old - m_new)
            p = jnp.exp(sc - m_new)
            l_ref[s] = alpha * l_ref[s] + jnp.sum(p, axis=-1, keepdims=True)
            m_ref[s] = m_new
            alphas.append(alpha)
            pvs.append(_dot(p.astype(BF16), vb))
        acc_ref[...] = (acc_ref[...] * jnp.where(lane < HEAD_DIM, alphas[0], alphas[1])
                        + jnp.where(lane < HEAD_DIM, pvs[0], pvs[1]))

    def body(kk, carry):
        step(kk, False)
        return carry

    lax.fori_loop(0, i, body, 0)
    step(i, True)
    o_ref[...] = acc_ref[...] / jnp.where(lane < HEAD_DIM, l_ref[0], l_ref[1])


def _fox_prompt(q, k, v, ct_col, ct_row):
    bsz, n, _ = q.shape
    blk = FOX_BLK
    tile = pl.BlockSpec((None, blk, PAIR), lambda b, j, i: (b, i, j))
    whole = pl.BlockSpec((None, n, PAIR), lambda b, j, i: (b, 0, j))
    return pl.pallas_call(
        _fox_prompt_kernel,
        grid=(bsz, N_PAIRS, n // blk),
        in_specs=[tile, whole, whole,
                  pl.BlockSpec((None, None, blk, 2), lambda b, j, i: (b, j, i, 0)),
                  pl.BlockSpec((None, None, 2, n), lambda b, j, i: (b, j, 0, 0))],
        out_specs=tile,
        out_shape=jax.ShapeDtypeStruct((bsz, n, ATT_WIDTH), F32),
        scratch_shapes=[pltpu.VMEM((2, blk, 1), F32), pltpu.VMEM((2, blk, 1), F32), pltpu.VMEM((blk, PAIR), F32)],
        compiler_params=_cparams(3),
        name="fox_prompt",
    )(q, k, v, ct_col, ct_row)


FOX_KV_BLK = 1024


def _fox_sample_kernel(q_ref, kc_ref, vc_ref, kn_ref, vn_ref, cq_ref, ckc_ref, ckn_ref, o_ref, m_ref, l_ref, acc_ref):
    j = pl.program_id(1)
    nq = q_ref.shape[0]

    @pl.when(j == 0)
    def _():
        m_ref[...] = jnp.full_like(m_ref, NEG_INF)
        l_ref[...] = jnp.zeros_like(l_ref)
        acc_ref[...] = jnp.zeros_like(acc_ref)

    qs = _stack_heads(q_ref[...] * QK_SCALE).astype(BF16)
    cq = cq_ref[...]

    def absorb(sc, vb):
        m_old = m_ref[...]
        m_new = jnp.maximum(m_old, jnp.max(sc, axis=-1, keepdims=True))
        alpha = jnp.exp(m_old - m_new)
        p = jnp.exp(sc - m_new)
        l_ref[...] = alpha * l_ref[...] + jnp.sum(p, axis=-1, keepdims=True)
        acc_ref[...] = alpha * acc_ref[...] + _dot(p.astype(BF16), vb)
        m_ref[...] = m_new

    absorb(_dot_nt(qs, kc_ref[...].astype(BF16)) + (cq - _rows_per_head(ckc_ref[...], nq)), vc_ref[...].astype(BF16))

    @pl.when(j == pl.num_programs(1) - 1)
    def _():
        sc = _dot_nt(qs, kn_ref[...].astype(BF16)) + (cq - _rows_per_head(ckn_ref[...], nq))
        row = lax.broadcasted_iota(jnp.int32, sc.shape, 0)
        col = lax.broadcasted_iota(jnp.int32, sc.shape, 1)
        absorb(jnp.where(col <= row % nq, sc, NEG_INF), vn_ref[...].astype(BF16))
        o_ref[...] = _unstack_heads(acc_ref[...] / l_ref[...], nq)


def _fox_sample(q, k, v, k_cache, v_cache, cq, ck_cache, ck_new):
    bsz, nq, _ = q.shape
    npast = k_cache.shape[1]
    kvb = FOX_KV_BLK
    new = pl.BlockSpec((None, nq, ATT_WIDTH), lambda b, j: (b, 0, 0))
    old = pl.BlockSpec((None, kvb, ATT_WIDTH), lambda b, j: (b, j, 0))
    rows = FOX_HEADS * nq
    return pl.pallas_call(
        _fox_sample_kernel,
        grid=(bsz, npast // kvb),
        in_specs=[new, old, old, new, new,
                  pl.BlockSpec((None, rows, 1), lambda b, j: (b, 0, 0)),
                  pl.BlockSpec((None, FOX_HEADS, kvb), lambda b, j: (b, 0, j)),
                  pl.BlockSpec((None, FOX_HEADS, nq), lambda b, j: (b, 0, 0))],
        out_specs=new,
        out_shape=jax.ShapeDtypeStruct((bsz, nq, ATT_WIDTH), F32),
        scratch_shapes=[pltpu.VMEM((rows, 1), F32), pltpu.VMEM((rows, 1), F32), pltpu.VMEM((rows, ATT_WIDTH), F32)],
        compiler_params=_cparams(2),
        name="fox_sample",
    )(q, k_cache, v_cache, k, v, cq, ck_cache, ck_new)


LANES = 128


def _fox_prompt_mixer(fq, fk, fv, gates):
    bsz, n, _ = fq.shape
    logf_t = jnp.swapaxes(gates[..., :FOX_HEADS], 1, 2).reshape(bsz * FOX_HEADS, n)
    ct = _cumsum_lanes(logf_t, FOX_BLK).reshape(bsz, N_PAIRS, 2, n)
    return _fox_prompt(fq, fk, fv, jnp.swapaxes(ct, 2, 3), ct)


def _fox_sample_mixer(fq, fk, fv, gates, k_cache, v_cache, lf_cache):
    bsz, nq, _ = fq.shape
    npast = k_cache.shape[1]
    lf_all = jnp.concatenate([jnp.swapaxes(lf_cache, 1, 2), jnp.swapaxes(gates[..., :FOX_HEADS], 1, 2)], axis=2)
    total = npast + nq
    padded = -(-total // LANES) * LANES
    lb = next(c for c in (512, 384, 256, 128) if padded % c == 0)
    lf_all = jnp.pad(lf_all, ((0, 0), (0, 0), (0, padded - total))).reshape(bsz * FOX_HEADS, padded)
    ct = _cumsum_lanes(lf_all, lb).reshape(bsz, FOX_HEADS, padded)
    ck_new = ct[:, :, npast:total]
    return _fox_sample(fq, fk, fv, k_cache, v_cache, ck_new.reshape(bsz, FOX_HEADS * nq, 1), ct[:, :, :npast], ck_new)


TM_FFN = 1024
TM_PROJ = 512
PROMPT_CHUNK = 128
SSD_BLOCK_ROWS = 512


def _block_diag_ones(n, width):
    idx = np.arange(n) // width
    return jnp.asarray(idx[:, None] == idx[None, :], BF16)


def _cols(w, splits):
    cuts = np.concatenate([[0], np.cumsum(splits)])
    return [w[:, int(cuts[i]):int(cuts[i + 1])] for i in range(len(splits))]


def _pad_cols(w, width):
    return jnp.pad(w, ((0, 0), (0, width - w.shape[1])))


def _row(v, width=None):
    v = v.reshape(1, -1).astype(F32)
    return v if width is None else _pad_cols(v, width)


def kernel(x_prompt, x_sample, p_prompt, p_sample, state_ssd_conv, state_ssd, cache_band_k, cache_band_v, cache_fox_k, cache_fox_v, cache_fox_logf, state_mlstm_C, state_mlstm_n, state_mlstm_m, norm_ffn1, ffn1_wg, ffn1_wu, ffn1_wd, norm_mix, norm_ffn2, ffn2_wg, ffn2_wu, ffn2_wd, ple_proj, ple_norm, ple_gate_norm, ple_gate_w, even_w_in, even_w_out, ssd_conv_w, ssd_conv_b, ssd_a_log, ssd_dt_bias, ssd_d, ssd_norm, band_q_norm, band_k_norm, band_rel_bias, odd_w_in, odd_w_out, fox_q_norm, fox_k_norm, fox_f_bias, mlstm_i_bias, mlstm_f_bias, mlstm_norm):
    bp, sp, _ = x_prompt.shape
    bs, ss, _ = x_sample.shape
    depth = norm_ffn1.shape[0]
    n_heads = ATT_WIDTH // HEAD_DIM
    assert ss <= CHUNK and sp % max(SSD_BLOCK_ROWS, FOX_BLK, TM_FFN) == 0 and (bs * ss) % TM_PROJ == 0

    def bf(w):
        return w.astype(BF16)

    bd = _block_diag_ones(ATT_WIDTH, HEAD_DIM)
    hp = x_prompt.reshape(bp * sp, D_MODEL)
    hs = x_sample.reshape(bs * ss, D_MODEL)
    tm_s = min(TM_FFN, bs * ss)
    even_p, even_s, odd_p, odd_s = [], [], [], []

    for i in range(depth):
        j = i // 2
        w1 = (_row(norm_ffn1[i]), bf(ffn1_wg[i]), bf(ffn1_wu[i]), bf(ffn1_wd[i]))
        hp = _ffn1(hp, *w1, tm=TM_FFN)
        hs = _ffn1(hs, *w1, tm=tm_s)
        gm = _row(norm_mix[i])

        if i % 2 == 0:
            wz, wx, wdt, wq, wk, wv = _cols(even_w_in[j], EVEN_SPLITS)
            wproj = (bf(wz), bf(wx), bf(_pad_cols(wdt, GATE_LANES)), bf(wq), bf(wk), bf(wv),
                     _row(ssd_dt_bias[j], GATE_LANES), _row(jnp.tile(band_q_norm[j], n_heads)),
                     _row(jnp.tile(band_k_norm[j], n_heads)), bd)
            ssd_w = (ssd_conv_w[j], ssd_conv_b[j], ssd_a_log[j], ssd_d[j], ssd_norm[j])

            z, xbc, dt, q, k, v = [a.reshape(bp, sp, -1) for a in _proj_even(hp, gm, *wproj, tm=TM_PROJ)]
            ya_p, conv_p, st_p = _ssd_mixer(xbc, z, dt, jnp.zeros((bp, SSD_CONV - 1, SSD_CONV_DIM), F32),
                                            jnp.zeros((bp, SSD_HEADS, HEAD_DIM, SSD_STATE), F32), *ssd_w,
                                            tb=SSD_BLOCK_ROWS, t=PROMPT_CHUNK)
            ob_p = _band_prompt(q, k, v, _band_prompt_bias(band_rel_bias[j]))
            keep = min(BAND_PAST, sp)
            even_p.append((conv_p, st_p, k[:, sp - keep:].reshape(bp, keep, n_heads, HEAD_DIM),
                           v[:, sp - keep:].reshape(bp, keep, n_heads, HEAD_DIM)))

            z, xbc, dt, q, k, v = [a.reshape(bs, ss, -1) for a in _proj_even(hs, gm, *wproj, tm=TM_PROJ)]
            ya_s, conv_s, st_s = _ssd_mixer(xbc, z, dt, state_ssd_conv[j], state_ssd[j], *ssd_w, tb=ss, t=ss)
            nb = cache_band_k.shape[2]
            ob_s = _band_sample(q, k, v, cache_band_k[j].reshape(bs, nb, ATT_WIDTH), cache_band_v[j].reshape(bs, nb, ATT_WIDTH),
                                *_band_sample_bias(band_rel_bias[j], ss, nb))
            even_s.append((conv_s, st_s, k.reshape(bs, ss, n_heads, HEAD_DIM), v.reshape(bs, ss, n_heads, HEAD_DIM)))

            mix_p, mix_s = (ya_p, ob_p), (ya_s, ob_s)
            w_out = even_w_out[j]
        else:
            wfq, wfk, wfv, wff, wmq, wmk, wmv, wmi, wmf, wmo = _cols(odd_w_in[j], ODD_SPLITS)
            wgt = _pad_cols(jnp.concatenate([wff, wmi, wmf], axis=1), GATE_LANES)
            gtb = _row(jnp.concatenate([fox_f_bias[j], mlstm_i_bias[j], mlstm_f_bias[j]]), GATE_LANES)
            wproj = (bf(wfq), bf(wfk), bf(wfv), bf(wgt), bf(wmq), bf(wmk), bf(wmv), bf(wmo), gtb,
                     _row(jnp.tile(fox_q_norm[j], n_heads)), _row(jnp.tile(fox_k_norm[j], n_heads)), bd)

            fq, fk, fv, gt, mq, mk, mv, so = [a.reshape(bp, sp, -1) for a in _proj_odd(hp, gm, *wproj, tm=TM_PROJ)]
            oc_p = _fox_prompt_mixer(fq, fk, fv, gt)
            hm_p, c_p, n_p, m_p = _mlstm_mixer(mq, mk, mv, so, gt,
                                               jnp.zeros((bp, MLSTM_HEADS, MLSTM_HEAD_DIM, MLSTM_HEAD_DIM), F32),
                                               jnp.zeros((bp, MLSTM_HEADS, MLSTM_HEAD_DIM), F32),
                                               jnp.zeros((bp, MLSTM_HEADS), F32), mlstm_norm[j], t=PROMPT_CHUNK)
            odd_p.append((fk.reshape(bp, sp, n_heads, HEAD_DIM), fv.reshape(bp, sp, n_heads, HEAD_DIM),
                          gt[..., :FOX_HEADS], c_p, n_p, m_p))

            fq, fk, fv, gt, mq, mk, mv, so = [a.reshape(bs, ss, -1) for a in _proj_odd(hs, gm, *wproj, tm=TM_PROJ)]
            npast = cache_fox_k.shape[2]
            oc_s = _fox_sample_mixer(fq, fk, fv, gt, cache_fox_k[j].reshape(bs, npast, ATT_WIDTH),
                                     cache_fox_v[j].reshape(bs, npast, ATT_WIDTH), cache_fox_logf[j])
            hm_s, c_s, n_s, m_s = _mlstm_mixer(mq, mk, mv, so, gt, state_mlstm_C[j], state_mlstm_n[j], state_mlstm_m[j],
                                               mlstm_norm[j], t=ss)
            odd_s.append((fk.reshape(bs, ss, n_heads, HEAD_DIM), fv.reshape(bs, ss, n_heads, HEAD_DIM),
                          gt[..., :FOX_HEADS], c_s, n_s, m_s))

            mix_p, mix_s = (oc_p, hm_p), (oc_s, hm_s)
            w_out = odd_w_out[j]

        ka = mix_p[0].shape[-1]
        wt = (bf(w_out[:ka]), bf(w_out[ka:]), _row(norm_ffn2[i]), bf(ffn2_wg[i]), bf(ffn2_wu[i]), bf(ffn2_wd[i]),
              bf(ple_proj[i]), _row(ple_norm[i]), _row(ple_gate_norm[i]), bf(ple_gate_w[i]))
        hp = _tail(hp, mix_p[0].reshape(bp * sp, -1), mix_p[1].reshape(bp * sp, -1),
                   p_prompt[i].reshape(bp * sp, PLE_DIM), *wt, tm=TM_PROJ)
        hs = _tail(hs, mix_s[0].reshape(bs * ss, -1), mix_s[1].reshape(bs * ss, -1),
                   p_sample[i].reshape(bs * ss, PLE_DIM), *wt, tm=TM_PROJ)

    def stk(states, idx):
        return jnp.stack([s[idx] for s in states])

    return (hp.reshape(bp, sp, D_MODEL), hs.reshape(bs, ss, D_MODEL),
            stk(even_p, 0), stk(even_s, 0), stk(even_p, 1), stk(even_s, 1),
            stk(even_p, 2), stk(even_s, 2), stk(even_p, 3), stk(even_s, 3),
            stk(odd_p, 0), stk(odd_s, 0), stk(odd_p, 1), stk(odd_s, 1), stk(odd_p, 2), stk(odd_s, 2),
            stk(odd_p, 3), stk(odd_s, 3), stk(odd_p, 4), stk(odd_s, 4), stk(odd_p, 5), stk(odd_s, 5))
```

```python
import functools

import numpy as np
import jax
import jax.numpy as jnp
from jax import lax
from jax.experimental import pallas as pl
from jax.experimental.pallas import tpu as pltpu

F32 = jnp.float32
BF16 = jnp.bfloat16
EPS = 1e-6
NEG_INF = -1e30

D_MODEL = 1024
D_FF = 2816
PLE_DIM = 256
HEAD_DIM = 64
CHUNK = 64
SSD_HEADS = 16
SSD_D_INNER = 1024
SSD_CONV_DIM = 1280
SSD_CONV = 4
BAND_HEADS = 8
BAND_PAST = 512
REL_CLIP = 128
FOX_HEADS = 8
MLSTM_HEADS = 4
MLSTM_HEAD_DIM = 128
ATT_WIDTH = 512
GATE_LANES = 128
EVEN_SPLITS = (1024, 1280, 16, 512, 512, 512)
ODD_SPLITS = (512, 512, 512, 8, 512, 512, 512, 4, 4, 512)

V7X_VMEM_BYTES = 64 * 1024 * 1024
VMEM_LIMIT = 56 * 1024 * 1024
FF_CHUNK = 256


def _cparams(n_axes):
    return pltpu.CompilerParams(dimension_semantics=("arbitrary",) * n_axes,
                                vmem_limit_bytes=VMEM_LIMIT)


def _resident(shape):
    nd = len(shape)
    return pl.BlockSpec(shape, lambda *_: (0,) * nd, pipeline_mode=pl.Buffered(1))


def _dot(a, b):
    return jnp.dot(a, b, preferred_element_type=F32)


def _dot_nt(a, b):
    return lax.dot_general(a, b, (((1,), (1,)), ((), ())), preferred_element_type=F32)


def _dot_tn(a, b):
    return lax.dot_general(a, b, (((0,), (0,)), ((), ())), preferred_element_type=F32)


def _split3(x):
    hi = x.astype(BF16)
    r = x - hi.astype(F32)
    mid = r.astype(BF16)
    lo = (r - mid.astype(F32)).astype(BF16)
    return hi, mid, lo


def _dot3_l(x, w01):
    hi, mid, lo = _split3(x)
    return _dot(hi, w01) + _dot(mid, w01) + _dot(lo, w01)


def _dot3_r(w01, x):
    hi, mid, lo = _split3(x)
    return _dot(w01, hi) + _dot(w01, mid) + _dot(w01, lo)


def _rms(x, g):
    ms = jnp.mean(x * x, axis=-1, keepdims=True)
    return x * lax.rsqrt(ms + EPS) * g


def _sigmoid(x):
    return 1.0 / (1.0 + jnp.exp(-x))


def _silu(x):
    return x * _sigmoid(x)


def _softplus(x):
    return jnp.maximum(x, 0.0) + jnp.log(1.0 + jnp.exp(-jnp.abs(x)))


def _log_sigmoid(x):
    return jnp.minimum(x, 0.0) - jnp.log(1.0 + jnp.exp(-jnp.abs(x)))


def _lower_tri(n):
    r = lax.broadcasted_iota(jnp.int32, (n, n), 0)
    c = lax.broadcasted_iota(jnp.int32, (n, n), 1)
    return r >= c


def _ffn_half(x, g_ref, wg_ref, wu_ref, wd_ref, acc_ref):
    xn = _rms(x, g_ref[...]).astype(BF16)
    for c in range(D_FF // FF_CHUNK):
        sl = slice(c * FF_CHUNK, (c + 1) * FF_CHUNK)
        gate = _dot(xn, wg_ref[:, sl])
        up = _dot(xn, wu_ref[:, sl])
        a = (_silu(gate) * up).astype(BF16)
        d = _dot(a, wd_ref[sl, :])
        if c == 0:
            acc_ref[...] = d
        else:
            acc_ref[...] += d
    return 0.5 * acc_ref[...]


def _ffn1_kernel(h_ref, g_ref, wg_ref, wu_ref, wd_ref, o_ref, acc_ref):
    x = h_ref[...]
    o_ref[...] = x + _ffn_half(x, g_ref, wg_ref, wu_ref, wd_ref, acc_ref)


def _ffn1(h, g, wg, wu, wd, tm):
    n = h.shape[0]
    tok = pl.BlockSpec((tm, D_MODEL), lambda i: (i, 0))
    return pl.pallas_call(
        _ffn1_kernel,
        grid=(n // tm,),
        in_specs=[tok, _resident((1, D_MODEL)), _resident((D_MODEL, D_FF)),
                  _resident((D_MODEL, D_FF)), _resident((D_FF, D_MODEL))],
        out_specs=tok,
        out_shape=jax.ShapeDtypeStruct((n, D_MODEL), F32),
        scratch_shapes=[pltpu.VMEM((tm, D_MODEL), F32)],
        compiler_params=_cparams(1),
        name="ffn1",
    )(h, g, wg, wu, wd)


def _head_rms(x, bd_ref, g, width):
    x2 = x * x
    hi = x2.astype(BF16)
    lo = (x2 - hi.astype(F32)).astype(BF16)
    ms = (_dot(hi, bd_ref[...]) + _dot(lo, bd_ref[...])) * (1.0 / width)
    return x * lax.rsqrt(ms + EPS) * g


def _proj_even_kernel(h_ref, gm_ref, wz_ref, wx_ref, wdt_ref, wq_ref, wk_ref, wv_ref,
                      dtb_ref, gq_ref, gk_ref, bd_ref,
                      z_ref, xbc_ref, dt_ref, q_ref, k_ref, v_ref):
    u = _rms(h_ref[...], gm_ref[...]).astype(BF16)
    z_ref[...] = _dot(u, wz_ref[...])
    xbc_ref[...] = _dot(u, wx_ref[...])
    dt_ref[...] = _softplus(_dot(u, wdt_ref[...]) + dtb_ref[...])
    q_ref[...] = _head_rms(_dot(u, wq_ref[...]), bd_ref, gq_ref[...], HEAD_DIM)
    k_ref[...] = _head_rms(_dot(u, wk_ref[...]), bd_ref, gk_ref[...], HEAD_DIM)
    v_ref[...] = _dot(u, wv_ref[...])


def _proj_even(h, gm, wz, wx, wdt, wq, wk, wv, dtb, gq, gk, bd, tm):
    n = h.shape[0]

    def tok(c):
        return pl.BlockSpec((tm, c), lambda i: (i, 0))

    widths = (D_MODEL, SSD_CONV_DIM, GATE_LANES, ATT_WIDTH, ATT_WIDTH, ATT_WIDTH)
    return pl.pallas_call(
        _proj_even_kernel,
        grid=(n // tm,),
        in_specs=[tok(D_MODEL), _resident((1, D_MODEL)),
                  _resident(wz.shape), _resident(wx.shape), _resident(wdt.shape),
                  _resident(wq.shape), _resident(wk.shape), _resident(wv.shape),
                  _resident(dtb.shape), _resident(gq.shape), _resident(gk.shape), _resident(bd.shape)],
        out_specs=[tok(c) for c in widths],
        out_shape=[jax.ShapeDtypeStruct((n, c), F32) for c in widths],
        compiler_params=_cparams(1),
        name="proj_even",
    )(h, gm, wz, wx, wdt, wq, wk, wv, dtb, gq, gk, bd)


def _proj_odd_kernel(h_ref, gm_ref, wfq_ref, wfk_ref, wfv_ref, wgt_ref, wmq_ref, wmk_ref, wmv_ref, wmo_ref,
                     gtb_ref, gq_ref, gk_ref, bd_ref,
                     fq_ref, fk_ref, fv_ref, gt_ref, mq_ref, mk_ref, mv_ref, so_ref, *, kv_transposed):
    u = _rms(h_ref[...], gm_ref[...]).astype(BF16)
    fq_ref[...] = _head_rms(_dot(u, wfq_ref[...]), bd_ref, gq_ref[...], HEAD_DIM)
    fk = _head_rms(_dot(u, wfk_ref[...]), bd_ref, gk_ref[...], HEAD_DIM)
    fv = _dot(u, wfv_ref[...])
    fk_ref[...] = fk.T if kv_transposed else fk
    fv_ref[...] = fv.T if kv_transposed else fv
    pre = _dot(u, wgt_ref[...]) + gtb_ref[...]
    lane = lax.broadcasted_iota(jnp.int32, pre.shape, 1)
    is_input_gate = (lane >= FOX_HEADS) & (lane < FOX_HEADS + MLSTM_HEADS)
    gt_ref[...] = jnp.where(is_input_gate, pre, _log_sigmoid(pre))
    mq_ref[...] = _dot(u, wmq_ref[...])
    mk_ref[...] = _dot(u, wmk_ref[...]) * (MLSTM_HEAD_DIM ** -0.5)
    mv_ref[...] = _dot(u, wmv_ref[...])
    so_ref[...] = _sigmoid(_dot(u, wmo_ref[...]))


def _proj_odd(h, gm, wfq, wfk, wfv, wgt, wmq, wmk, wmv, wmo, gtb, gq, gk, bd, tm, seq=None):
    n = h.shape[0]

    def tok(c):
        return pl.BlockSpec((tm, c), lambda i: (i, 0))

    widths = (ATT_WIDTH, ATT_WIDTH, ATT_WIDTH, GATE_LANES, ATT_WIDTH, ATT_WIDTH, ATT_WIDTH, ATT_WIDTH)
    out_specs = [tok(c) for c in widths]
    out_shape = [jax.ShapeDtypeStruct((n, c), F32) for c in widths]
    if seq is not None:
        per_seq = seq // tm
        for o in (1, 2):
            out_specs[o] = pl.BlockSpec((None, ATT_WIDTH, tm), lambda i: (i // per_seq, 0, i % per_seq))
            out_shape[o] = jax.ShapeDtypeStruct((n // seq, ATT_WIDTH, seq), F32)
    ws = (wfq, wfk, wfv, wgt, wmq, wmk, wmv, wmo, gtb, gq, gk, bd)
    return pl.pallas_call(
        functools.partial(_proj_odd_kernel, kv_transposed=seq is not None),
        grid=(n // tm,),
        in_specs=[tok(D_MODEL), _resident((1, D_MODEL))] + [_resident(w.shape) for w in ws],
        out_specs=out_specs,
        out_shape=out_shape,
        compiler_params=_cparams(1),
        name="proj_odd",
    )(h, gm, *ws)


def _tail_kernel(h_ref, a_ref, b_ref, p_ref, woa_ref, wob_ref, g2_ref, wg_ref, wu_ref, wd_ref,
                 wple_ref, gple_ref, ggate_ref, wgate_ref, o_ref, acc_ref):
    h = h_ref[...] + _dot(a_ref[...].astype(BF16), woa_ref[...]) + _dot(b_ref[...].astype(BF16), wob_ref[...])
    h = h + _ffn_half(h, g2_ref, wg_ref, wu_ref, wd_ref, acc_ref)
    e = _rms(_dot(p_ref[...].astype(BF16), wple_ref[...]), gple_ref[...])
    gate = _sigmoid(_dot(_rms(h, ggate_ref[...]).astype(BF16), wgate_ref[...]))
    o_ref[...] = h + e * gate


def _tail(h, a, b, p, woa, wob, g2, wg, wu, wd, wple, gple, ggate, wgate, tm):
    n = h.shape[0]

    def tok(c):
        return pl.BlockSpec((tm, c), lambda i: (i, 0))

    ws = (woa, wob, g2, wg, wu, wd, wple, gple, ggate, wgate)
    return pl.pallas_call(
        _tail_kernel,
        grid=(n // tm,),
        in_specs=[tok(D_MODEL), tok(a.shape[1]), tok(b.shape[1]), tok(PLE_DIM)] + [_resident(w.shape) for w in ws],
        out_specs=tok(D_MODEL),
        out_shape=jax.ShapeDtypeStruct((n, D_MODEL), F32),
        scratch_shapes=[pltpu.VMEM((tm, D_MODEL), F32)],
        compiler_params=_cparams(1),
        name="tail",
    )(h, a, b, p, *ws)


CONV_PAD = 8
SSD_PAIR = 128
SSD_GROUP_COLS = 512
SSD_STATE = 64


def _ssd_kernel(xbc_ref, z_ref, dt_ref, dtT_ref, conv0_ref, st0_ref, cw_ref, cb_ref,
                alog_ref, alogT_ref, dskip_ref, gn_ref, e_ref,
                y_ref, convo_ref, sto_ref, xp_ref, st_ref, *, tb, t):
    i = pl.program_id(1)

    @pl.when(i == 0)
    def _():
        xp_ref[0:CONV_PAD, :] = conv0_ref[...]
        st_ref[...] = jnp.zeros_like(st_ref)
        st_ref[0:SSD_STATE, 0:SSD_GROUP_COLS] = st0_ref[0]
        st_ref[SSD_STATE:2 * SSD_STATE, SSD_GROUP_COLS:2 * SSD_GROUP_COLS] = st0_ref[1]

    xp_ref[CONV_PAD:CONV_PAD + tb, :] = xbc_ref[...]
    cw = cw_ref[...]
    pre = cb_ref[...]
    for j in range(SSD_CONV):
        off = CONV_PAD - (SSD_CONV - 1) + j
        pre = pre + cw[j:j + 1, :] * xp_ref[off:off + tb, :]
    xc = _silu(pre)
    tail_rows = xp_ref[tb:tb + CONV_PAD, :]
    convo_ref[...] = tail_rows
    xp_ref[0:CONV_PAD, :] = tail_rows

    tri = _lower_tri(t)
    ltri = tri.astype(BF16)
    utri = (lax.broadcasted_iota(jnp.int32, (t, t), 0) <= lax.broadcasted_iota(jnp.int32, (t, t), 1)).astype(BF16)
    a_row = -jnp.exp(alog_ref[...])
    a_col = -jnp.exp(alogT_ref[...])
    e01 = e_ref[...]
    lane = lax.broadcasted_iota(jnp.int32, (t, SSD_PAIR), 1)
    st_r = lax.broadcasted_iota(jnp.int32, st_ref.shape, 0)
    st_c = lax.broadcasted_iota(jnp.int32, st_ref.shape, 1)
    st_own = (st_r < SSD_STATE) == (st_c < SSD_GROUP_COLS)

    for c in range(tb // t):
        r0 = c * t
        xs = xc[r0:r0 + t, 0:SSD_D_INNER]
        b2 = xc[r0:r0 + t, SSD_D_INNER:SSD_D_INNER + 128]
        c2 = xc[r0:r0 + t, SSD_D_INNER + 128:SSD_D_INNER + 256]
        dtc = dt_ref[r0:r0 + t, :]
        acum = _dot3_r(ltri, dtc * a_row)
        acum_t = _dot3_l(dtT_ref[:, r0:r0 + t] * a_col, utri)
        acum_x = _dot3_l(acum, e01)
        xdt = xs * _dot3_l(dtc, e01)
        last = acum_x[t - 1:t, :]
        tailx = jnp.exp(last - acum_x) * xdt
        b2b = b2.astype(BF16)
        y = _dot(c2.astype(BF16), st_ref[...].astype(BF16)) * jnp.exp(acum_x) + dskip_ref[...] * xs
        cbs = []
        for g in range(2):
            in_g = (lane >= HEAD_DIM * g) & (lane < HEAD_DIM * (g + 1))
            cbs.append(_dot_nt(jnp.where(in_g, c2, 0.0).astype(BF16), b2b))
        ys = []
        for j in range(SSD_HEADS // 2):
            xpair = xdt[:, SSD_PAIR * j:SSD_PAIR * (j + 1)]
            acc = None
            for s in range(2):
                hd = 2 * j + s
                seg = acum[:, hd:hd + 1] - acum_t[hd:hd + 1, :]
                m = jnp.exp(jnp.where(tri, seg, NEG_INF)) * cbs[j // 4]
                xh = jnp.where((lane < HEAD_DIM) if s == 0 else (lane >= HEAD_DIM), xpair, 0.0)
                d = _dot(m.astype(BF16), xh.astype(BF16))
                acc = d if acc is None else acc + d
            ys.append(acc)
        y = y + jnp.concatenate(ys, axis=1)
        upd = _dot_tn(b2b, tailx.astype(BF16))
        st_ref[...] = jnp.where(st_own, st_ref[...] * jnp.exp(last) + upd, 0.0)
        y_ref[r0:r0 + t, :] = _rms(y * _silu(z_ref[r0:r0 + t, :]), gn_ref[...])

    sto_ref[0] = st_ref[0:SSD_STATE, 0:SSD_GROUP_COLS]
    sto_ref[1] = st_ref[SSD_STATE:2 * SSD_STATE, SSD_GROUP_COLS:2 * SSD_GROUP_COLS]


def _ssd(xbc, z, dt, dt_t, conv0, st0, cw, cb, alog, alog_t, dskip, gn, e01, tb, t):
    bsz, n, _ = xbc.shape
    consts = (cw, cb, alog, alog_t, dskip, gn, e01)
    return pl.pallas_call(
        functools.partial(_ssd_kernel, tb=tb, t=t),
        grid=(bsz, n // tb),
        in_specs=[pl.BlockSpec((None, tb, SSD_CONV_DIM), lambda b, i: (b, i, 0)),
                  pl.BlockSpec((None, tb, SSD_D_INNER), lambda b, i: (b, i, 0)),
                  pl.BlockSpec((None, tb, GATE_LANES), lambda b, i: (b, i, 0)),
                  pl.BlockSpec((None, SSD_HEADS, tb), lambda b, i: (b, 0, i)),
                  pl.BlockSpec((None, CONV_PAD, SSD_CONV_DIM), lambda b, i: (b, 0, 0)),
                  pl.BlockSpec((None, 2, SSD_STATE, SSD_GROUP_COLS), lambda b, i: (b, 0, 0, 0))]
                 + [_resident(w.shape) for w in consts],
        out_specs=[pl.BlockSpec((None, tb, SSD_D_INNER), lambda b, i: (b, i, 0)),
                   pl.BlockSpec((None, CONV_PAD, SSD_CONV_DIM), lambda b, i: (b, 0, 0)),
                   pl.BlockSpec((None, 2, SSD_STATE, SSD_GROUP_COLS), lambda b, i: (b, 0, 0, 0))],
        out_shape=[jax.ShapeDtypeStruct((bsz, n, SSD_D_INNER), F32),
                   jax.ShapeDtypeStruct((bsz, CONV_PAD, SSD_CONV_DIM), F32),
                   jax.ShapeDtypeStruct((bsz, 2, SSD_STATE, SSD_GROUP_COLS), F32)],
        scratch_shapes=[pltpu.VMEM((CONV_PAD + tb, SSD_CONV_DIM), F32),
                        pltpu.VMEM((2 * SSD_STATE, 2 * SSD_GROUP_COLS), F32)],
        compiler_params=_cparams(2),
        name="ssd",
    )(xbc, z, dt, dt_t, conv0, st0, *consts)


def _head_expand(n_heads, width):
    e = np.zeros((GATE_LANES, n_heads * width), np.float32)
    for h in range(n_heads):
        e[h, h * width:(h + 1) * width] = 1.0
    return jnp.asarray(e, BF16)


def _ssd_mixer(xbc, z, dt, conv_buf, s0, conv_w, conv_b, a_log, d_skip, ssd_norm, tb, t):
    bsz = xbc.shape[0]
    dt_t = jnp.swapaxes(dt[..., :SSD_HEADS], 1, 2)
    conv0 = jnp.pad(conv_buf, ((0, 0), (CONV_PAD - (SSD_CONV - 1), 0), (0, 0)))
    st0 = s0.reshape(bsz, 2, 8, HEAD_DIM, SSD_STATE).transpose(0, 1, 4, 2, 3).reshape(bsz, 2, SSD_STATE, SSD_GROUP_COLS)
    cw = jnp.pad(conv_w, ((0, CONV_PAD - SSD_CONV), (0, 0)))
    alog = jnp.pad(a_log, (0, GATE_LANES - SSD_HEADS)).reshape(1, GATE_LANES)
    y, convo, sto = _ssd(xbc, z, dt, dt_t, conv0, st0, cw, conv_b.reshape(1, -1), alog, a_log.reshape(SSD_HEADS, 1),
                         jnp.repeat(d_skip, HEAD_DIM).reshape(1, -1), ssd_norm.reshape(1, -1),
                         _head_expand(SSD_HEADS, HEAD_DIM), tb, t)
    s_new = sto.reshape(bsz, 2, SSD_STATE, 8, HEAD_DIM).transpose(0, 1, 3, 4, 2).reshape(bsz, SSD_HEADS, HEAD_DIM, SSD_STATE)
    return y, convo[:, CONV_PAD - (SSD_CONV - 1):], s_new


GATE_I = FOX_HEADS
GATE_F = FOX_HEADS + MLSTM_HEADS


def _mlstm_kernel(q_ref, k_ref, v_ref, so_ref, gt_ref, gtT_ref, c0_ref, n0_ref, m0_ref, gn_ref,
                  h_ref, co_ref, no_ref, mo_ref, c_ref, n_ref, m_ref, *, t):
    i = pl.program_id(1)

    @pl.when(i == 0)
    def _():
        c_ref[...] = c0_ref[...]
        n_ref[...] = n0_ref[...]
        m_ref[...] = m0_ref[...]

    tri = _lower_tri(t)
    ltri = tri.astype(BF16)
    utri = (lax.broadcasted_iota(jnp.int32, (t, t), 0) <= lax.broadcasted_iota(jnp.int32, (t, t), 1)).astype(BF16)
    gt = gt_ref[...]
    gt_t = gtT_ref[...]
    cum = _dot3_r(ltri, gt)
    cum_t = _dot3_l(gt_t, utri)
    m_vec = m_ref[...]
    m_lane = lax.broadcasted_iota(jnp.int32, m_vec.shape, 1)
    m_next = m_vec
    for hh in range(MLSTM_HEADS):
        sl = slice(MLSTM_HEAD_DIM * hh, MLSTM_HEAD_DIM * (hh + 1))
        bcol = cum[:, GATE_F + hh:GATE_F + hh + 1]
        brow = cum_t[GATE_F + hh:GATE_F + hh + 1, :]
        ig_row = gt_t[GATE_I + hh:GATE_I + hh + 1, :]
        ig_col = gt[:, GATE_I + hh:GATE_I + hh + 1]
        dmat = jnp.where(tri, bcol - brow + ig_row, NEG_INF)
        g = bcol + m_vec[:, hh:hh + 1]
        m_t = jnp.maximum(g, jnp.max(dmat, axis=-1, keepdims=True))
        w = jnp.exp(dmat - m_t)
        inter = jnp.exp(g - m_t)
        qh = q_ref[:, sl]
        kh = k_ref[:, sl]
        vh = v_ref[:, sl]
        qb = qh.astype(BF16)
        kb = kh.astype(BF16)
        a = w * _dot_nt(qb, kb)
        cm = c_ref[hh]
        nv = n_ref[hh:hh + 1, :]
        num = inter * _dot(qb, cm.astype(BF16)) + _dot(a.astype(BF16), vh.astype(BF16))
        den = inter * jnp.sum(qh * nv, axis=-1, keepdims=True) + jnp.sum(a, axis=-1, keepdims=True)
        hout = num / jnp.maximum(jnp.abs(den), jnp.exp(-m_t))
        m_last = m_t[t - 1:t, :]
        inter_last = inter[t - 1:t, :]
        w_end = jnp.exp(bcol[t - 1:t, :] - bcol + ig_col - m_last)
        c_ref[hh] = inter_last * cm + _dot_tn(kb, (w_end * vh).astype(BF16))
        n_ref[hh:hh + 1, :] = inter_last * nv + jnp.sum(w_end * kh, axis=0, keepdims=True)
        m_next = jnp.where(m_lane == hh, m_last, m_next)
        h_ref[:, sl] = _rms(hout, gn_ref[:, sl]) * so_ref[:, sl]
    m_ref[...] = m_next
    co_ref[...] = c_ref[...]
    no_ref[...] = n_ref[...]
    mo_ref[...] = m_next


def _mlstm(q, k, v, so, gt, gt_t, c0, n0, m0, gn, t):
    bsz, n, _ = q.shape
    tok = pl.BlockSpec((None, t, ATT_WIDTH), lambda b, i: (b, i, 0))
    c_spec = pl.BlockSpec((None, MLSTM_HEADS, MLSTM_HEAD_DIM, MLSTM_HEAD_DIM), lambda b, i: (b, 0, 0, 0))
    n_spec = pl.BlockSpec((None, MLSTM_HEADS, MLSTM_HEAD_DIM), lambda b, i: (b, 0, 0))
    m_spec = pl.BlockSpec((None, 1, GATE_LANES), lambda b, i: (b, 0, 0))
    return pl.pallas_call(
        functools.partial(_mlstm_kernel, t=t),
        grid=(bsz, n // t),
        in_specs=[tok, tok, tok, tok,
                  pl.BlockSpec((None, t, GATE_LANES), lambda b, i: (b, i, 0)),
                  pl.BlockSpec((None, 16, t), lambda b, i: (b, 0, i)),
                  c_spec, n_spec, m_spec, _resident(gn.shape)],
        out_specs=[tok, c_spec, n_spec, m_spec],
        out_shape=[jax.ShapeDtypeStruct((bsz, n, ATT_WIDTH), F32),
                   jax.ShapeDtypeStruct(c0.shape, F32),
                   jax.ShapeDtypeStruct(n0.shape, F32),
                   jax.ShapeDtypeStruct(m0.shape, F32)],
        scratch_shapes=[pltpu.VMEM((MLSTM_HEADS, MLSTM_HEAD_DIM, MLSTM_HEAD_DIM), F32),
                        pltpu.VMEM((MLSTM_HEADS, MLSTM_HEAD_DIM), F32),
                        pltpu.VMEM((1, GATE_LANES), F32)],
        compiler_params=_cparams(2),
        name="mlstm",
    )(q, k, v, so, gt, gt_t, c0, n0, m0, gn)


def _mlstm_mixer(mq, mk, mv, so, gates, c0, n0, m0, ml_norm, t):
    gt_t = jnp.swapaxes(gates[..., :16], 1, 2)
    m0p = jnp.pad(m0, ((0, 0), (0, GATE_LANES - MLSTM_HEADS)))[:, None, :]
    h, c_new, n_new, m_new = _mlstm(mq, mk, mv, so, gates, gt_t, c0, n0, m0p, ml_norm.reshape(1, -1), t)
    return h, c_new, n_new, m_new[:, 0, :MLSTM_HEADS]


PAIR = 2 * HEAD_DIM
N_PAIRS = ATT_WIDTH // PAIR
QK_SCALE = HEAD_DIM ** -0.5


def _in_head(lane, h):
    return (lane >= HEAD_DIM * h) & (lane < HEAD_DIM * (h + 1))


def _stack_heads(q):
    lane = lax.broadcasted_iota(jnp.int32, q.shape, 1)
    return jnp.concatenate([jnp.where(_in_head(lane, h), q, 0.0) for h in range(ATT_WIDTH // HEAD_DIM)], axis=0)


def _unstack_heads(o, nq):
    lane = lax.broadcasted_iota(jnp.int32, (nq, ATT_WIDTH), 1)
    out = o[0:nq, :]
    for h in range(1, ATT_WIDTH // HEAD_DIM):
        out = jnp.where(_in_head(lane, h), o[h * nq:(h + 1) * nq, :], out)
    return out


def _rows_per_head(x_t, nq):
    n = x_t.shape[1]
    return jnp.concatenate([jnp.broadcast_to(x_t[h:h + 1, :], (nq, n)) for h in range(x_t.shape[0])], axis=0)


BAND_QB = BAND_PAST // 2


def _band_prompt_kernel(q_ref, k2_ref, k1_ref, k0_ref, v2_ref, v1_ref, v0_ref, bias_ref, o_ref):
    i = pl.program_id(2)
    qb = q_ref.shape[0]
    lane = lax.broadcasted_iota(jnp.int32, (qb, PAIR), 1)
    q = q_ref[...] * QK_SCALE
    kcat = jnp.concatenate([k2_ref[...], k1_ref[...], k0_ref[...]], axis=0).astype(BF16)
    vcat = jnp.concatenate([v2_ref[...], v1_ref[...], v0_ref[...]], axis=0).astype(BF16)
    col = lax.broadcasted_iota(jnp.int32, (qb, 3 * qb), 1)
    in_sequence = col >= (2 - i) * qb
    outs = []
    for s in range(2):
        qh = jnp.where(_in_head(lane, s), q, 0.0).astype(BF16)
        sc = jnp.where(in_sequence, _dot_nt(qh, kcat) + bias_ref[s], NEG_INF)
        p = jnp.exp(sc - jnp.max(sc, axis=-1, keepdims=True))
        outs.append(_dot(p.astype(BF16), vcat) / jnp.sum(p, axis=-1, keepdims=True))
    o_ref[...] = jnp.where(lane < HEAD_DIM, outs[0], outs[1])


def _band_prompt(q, k, v, bias):
    bsz, n, _ = q.shape
    qb = BAND_QB

    def kv_spec(back):
        return pl.BlockSpec((None, qb, PAIR), lambda b, j, i: (b, jnp.maximum(i - back, 0), j))

    return pl.pallas_call(
        _band_prompt_kernel,
        grid=(bsz, N_PAIRS, n // qb),
        in_specs=[kv_spec(0), kv_spec(2), kv_spec(1), kv_spec(0), kv_spec(2), kv_spec(1), kv_spec(0),
                  pl.BlockSpec((2, qb, 3 * qb), lambda b, j, i: (j, 0, 0))],
        out_specs=kv_spec(0),
        out_shape=jax.ShapeDtypeStruct((bsz, n, ATT_WIDTH), F32),
        compiler_params=_cparams(3),
        name="band_prompt",
    )(q, k, k, k, v, v, v, bias)


def _rel_bias_block(rel_table, rows, cols, offset):
    period = rows + cols - 1
    dist = offset + rows - 1 - np.arange(period)
    profile = rel_table[:, np.clip(dist, -REL_CLIP, REL_CLIP) + REL_CLIP].astype(F32)
    rolled = jnp.roll(profile, -(rows - 1), axis=1)
    flat = jnp.tile(rolled, (1, rows))[:, :rows * (period - 1)]
    return flat.reshape(-1, rows, period - 1)[:, :, :cols]


def _band_prompt_bias(rel_table):
    qb = BAND_QB
    r = np.arange(qb)[:, None] + 2 * qb
    c = np.arange(3 * qb)[None, :]
    in_band = (c // CHUNK <= r // CHUNK) & (c // CHUNK >= r // CHUNK - BAND_PAST // CHUNK)
    return jnp.where(jnp.asarray(in_band)[None], _rel_bias_block(rel_table, qb, 3 * qb, 2 * qb), NEG_INF)


def _band_sample_kernel(q_ref, kct_ref, vct_ref, kn_ref, vn_ref, bc_ref, bn_ref, o_ref):
    nq = q_ref.shape[0]
    qs = _stack_heads(q_ref[...] * QK_SCALE).astype(BF16)
    s_c = _dot(qs, kct_ref[...].astype(BF16)) + bc_ref[...]
    s_n = _dot_nt(qs, kn_ref[...].astype(BF16)) + bn_ref[...]
    m = jnp.maximum(jnp.max(s_c, axis=-1, keepdims=True), jnp.max(s_n, axis=-1, keepdims=True))
    p_c = jnp.exp(s_c - m)
    p_n = jnp.exp(s_n - m)
    l = jnp.sum(p_c, axis=-1, keepdims=True) + jnp.sum(p_n, axis=-1, keepdims=True)
    o = _dot_nt(p_c.astype(BF16), vct_ref[...].astype(BF16)) + _dot(p_n.astype(BF16), vn_ref[...].astype(BF16))
    o_ref[...] = _unstack_heads(o / l, nq)


def _band_sample(q, k, v, kt_cache, vt_cache, bias_c, bias_n):
    bsz, nq, _ = q.shape
    nb = kt_cache.shape[2]
    new = pl.BlockSpec((None, nq, ATT_WIDTH), lambda b: (b, 0, 0))
    old = pl.BlockSpec((None, ATT_WIDTH, nb), lambda b: (b, 0, 0))
    return pl.pallas_call(
        _band_sample_kernel,
        grid=(bsz,),
        in_specs=[new, old, old, new, new, _resident(bias_c.shape), _resident(bias_n.shape)],
        out_specs=new,
        out_shape=jax.ShapeDtypeStruct((bsz, nq, ATT_WIDTH), F32),
        compiler_params=_cparams(1),
        name="band_sample",
    )(q, kt_cache, vt_cache, k, v, bias_c, bias_n)


def _band_sample_bias(rel_table, nq, nb):
    bias = _rel_bias_block(rel_table, nq, nb + nq, nb).reshape(BAND_HEADS * nq, nb + nq)
    return bias[:, :nb], bias[:, nb:]


def _cumsum_lanes_kernel(x_ref, o_ref, carry_ref):
    lb = x_ref.shape[1]

    @pl.when(pl.program_id(0) == 0)
    def _():
        carry_ref[...] = jnp.zeros_like(carry_ref)

    utri = (lax.broadcasted_iota(jnp.int32, (lb, lb), 0) <= lax.broadcasted_iota(jnp.int32, (lb, lb), 1)).astype(BF16)
    c = _dot3_l(x_ref[...], utri) + carry_ref[...]
    o_ref[...] = c
    carry_ref[...] = c[:, lb - 1:lb]


def _cumsum_lanes(x, lb):
    rows, n = x.shape
    blk = pl.BlockSpec((rows, lb), lambda i: (0, i))
    return pl.pallas_call(
        _cumsum_lanes_kernel,
        grid=(n // lb,),
        in_specs=[blk],
        out_specs=blk,
        out_shape=jax.ShapeDtypeStruct((rows, n), F32),
        scratch_shapes=[pltpu.VMEM((rows, 1), F32)],
        compiler_params=_cparams(1),
        name="cumsum_lanes",
    )(x)


FOX_BLK = 512
FOX_UNROLL = 4


LOG2E = 1.4426950408889634


def _fox_prompt_kernel(q_ref, kt_ref, vt_ref, cq_ref, ck_ref, o_ref, m_ref, acc_ref):
    i = pl.program_id(2)
    blk = q_ref.shape[0]
    lane = lax.broadcasted_iota(jnp.int32, (blk, PAIR), 1)
    q = q_ref[...] * (QK_SCALE * LOG2E)
    q2 = jnp.concatenate([jnp.where(_in_head(lane, s), q, 0.0) for s in range(2)], axis=0).astype(BF16)
    cq = cq_ref[...] * LOG2E
    cqb = jnp.concatenate([jnp.broadcast_to(cq[:, s:s + 1], (blk, PAIR)) for s in range(2)], axis=0)
    tri = _lower_tri(blk)
    tri2 = jnp.concatenate([tri, tri], axis=0)
    ones = jnp.ones((PAIR, blk), BF16)
    m_ref[...] = jnp.full_like(m_ref, NEG_INF)
    acc_ref[...] = jnp.zeros_like(acc_ref)

    def step(kk, on_diagonal):
        k0 = pl.multiple_of(kk * blk, blk)
        kb = kt_ref[:, pl.ds(k0, blk)].astype(BF16)
        vb = jnp.concatenate([vt_ref[:, pl.ds(k0, blk)].astype(BF16), ones], axis=0)
        ck = ck_ref[:, pl.ds(k0, blk)] * LOG2E
        sc = _dot(q2, kb)
        sc = jnp.concatenate([sc[:blk] - ck[0:1], sc[blk:] - ck[1:2]], axis=0)
        if on_diagonal:
            sc = jnp.where(tri2, sc, NEG_INF)
        chunks = [sc[:, PAIR * c:PAIR * (c + 1)] for c in range(blk // PAIR)]
        rm = chunks[0]
        for ch in chunks[1:]:
            rm = jnp.maximum(rm, ch)
        rmb = jnp.broadcast_to(jnp.max(rm, axis=-1, keepdims=True), (2 * blk, PAIR))
        m_old = m_ref[...]
        m_new = jnp.maximum(m_old, rmb + cqb)
        shift = m_new - cqb
        p = jnp.concatenate([jnp.exp2(ch - shift) for ch in chunks], axis=1).astype(BF16)
        alpha = jnp.exp2(m_old - m_new)
        acc_ref[...] = jnp.concatenate([alpha, alpha], axis=1) * acc_ref[...] + _dot_nt(p, vb)
        m_ref[...] = m_new

    def body(quad, carry):
        for u in range(FOX_UNROLL):
            step(FOX_UNROLL * quad + u, False)
        return carry

    lax.fori_loop(0, lax.shift_right_logical(i, FOX_UNROLL.bit_length() - 1), body, 0)
    done = jnp.bitwise_and(i, -FOX_UNROLL)
    width = FOX_UNROLL // 2
    while width >= 1:
        @pl.when(jnp.bitwise_and(i, width) != 0)
        def _(done=done, width=width):
            for u in range(width):
                step(done + u, False)
        done = done + jnp.bitwise_and(i, width)
        width //= 2

    step(i, True)
    acc = acc_ref[...]
    o = acc[:, :PAIR] / acc[:, PAIR:]
    o_ref[...] = jnp.where(lane < HEAD_DIM, o[:blk], o[blk:])


def _fox_prompt(q, kt, vt, ct_col, ct_row):
    bsz, n, _ = q.shape
    blk = FOX_BLK
    tile = pl.BlockSpec((None, blk, PAIR), lambda b, j, i: (b, i, j))
    whole = pl.BlockSpec((None, PAIR, n), lambda b, j, i: (b, j, 0))
    return pl.pallas_call(
        _fox_prompt_kernel,
        grid=(bsz, N_PAIRS, n // blk),
        in_specs=[tile, whole, whole,
                  pl.BlockSpec((None, None, blk, 2), lambda b, j, i: (b, j, i, 0)),
                  pl.BlockSpec((None, None, 2, n), lambda b, j, i: (b, j, 0, 0))],
        out_specs=tile,
        out_shape=jax.ShapeDtypeStruct((bsz, n, ATT_WIDTH), F32),
        scratch_shapes=[pltpu.VMEM((2 * blk, PAIR), F32), pltpu.VMEM((2 * blk, 2 * PAIR), F32)],
        compiler_params=_cparams(3),
        name="fox_prompt",
    )(q, kt, vt, ct_col, ct_row)


FOX_KV_BLK = 1024


def _fox_sample_kernel(q_ref, kct_ref, vct_ref, kn_ref, vn_ref, cq_ref, ckc_ref, ckn_ref, o_ref, m_ref, l_ref, acc_ref):
    j = pl.program_id(1)
    nq = q_ref.shape[0]

    @pl.when(j == 0)
    def _():
        m_ref[...] = jnp.full_like(m_ref, NEG_INF)
        l_ref[...] = jnp.zeros_like(l_ref)
        acc_ref[...] = jnp.zeros_like(acc_ref)

    qs = _stack_heads(q_ref[...] * QK_SCALE).astype(BF16)
    cq = cq_ref[...]

    def absorb(sc, pv):
        m_old = m_ref[...]
        m_new = jnp.maximum(m_old, jnp.max(sc, axis=-1, keepdims=True))
        alpha = jnp.exp(m_old - m_new)
        p = jnp.exp(sc - m_new)
        l_ref[...] = alpha * l_ref[...] + jnp.sum(p, axis=-1, keepdims=True)
        acc_ref[...] = alpha * acc_ref[...] + pv(p.astype(BF16))
        m_ref[...] = m_new

    vct = vct_ref[...].astype(BF16)
    absorb(_dot(qs, kct_ref[...].astype(BF16)) + (cq - _rows_per_head(ckc_ref[...], nq)), lambda p: _dot_nt(p, vct))

    @pl.when(j == pl.num_programs(1) - 1)
    def _():
        sc = _dot_nt(qs, kn_ref[...].astype(BF16)) + (cq - _rows_per_head(ckn_ref[...], nq))
        row = lax.broadcasted_iota(jnp.int32, sc.shape, 0)
        col = lax.broadcasted_iota(jnp.int32, sc.shape, 1)
        vn = vn_ref[...].astype(BF16)
        absorb(jnp.where(col <= row % nq, sc, NEG_INF), lambda p: _dot(p, vn))
        o_ref[...] = _unstack_heads(acc_ref[...] / l_ref[...], nq)


def _fox_sample(q, k, v, kt_cache, vt_cache, cq, ck_cache, ck_new):
    bsz, nq, _ = q.shape
    npast = kt_cache.shape[2]
    kvb = FOX_KV_BLK
    new = pl.BlockSpec((None, nq, ATT_WIDTH), lambda b, j: (b, 0, 0))
    old = pl.BlockSpec((None, ATT_WIDTH, kvb), lambda b, j: (b, 0, j))
    rows = FOX_HEADS * nq
    return pl.pallas_call(
        _fox_sample_kernel,
        grid=(bsz, npast // kvb),
        in_specs=[new, old, old, new, new,
                  pl.BlockSpec((None, rows, 1), lambda b, j: (b, 0, 0)),
                  pl.BlockSpec((None, FOX_HEADS, kvb), lambda b, j: (b, 0, j)),
                  pl.BlockSpec((None, FOX_HEADS, nq), lambda b, j: (b, 0, 0))],
        out_specs=new,
        out_shape=jax.ShapeDtypeStruct((bsz, nq, ATT_WIDTH), F32),
        scratch_shapes=[pltpu.VMEM((rows, 1), F32), pltpu.VMEM((rows, 1), F32), pltpu.VMEM((rows, ATT_WIDTH), F32)],
        compiler_params=_cparams(2),
        name="fox_sample",
    )(q, kt_cache, vt_cache, k, v, cq, ck_cache, ck_new)


LANES = 128


def _fox_prompt_mixer(fq, fk, fv, gates):
    bsz, n, _ = fq.shape
    logf_t = jnp.swapaxes(gates[..., :FOX_HEADS], 1, 2).reshape(bsz * FOX_HEADS, n)
    ct = _cumsum_lanes(logf_t, FOX_BLK).reshape(bsz, N_PAIRS, 2, n)
    return _fox_prompt(fq, fk, fv, jnp.swapaxes(ct, 2, 3), ct)


def _cache_t(cache):
    bsz, npast = cache.shape[:2]
    return jnp.transpose(cache, (0, 2, 3, 1)).reshape(bsz, ATT_WIDTH, npast)


def _fox_sample_mixer(fq, fk, fv, gates, k_cache, v_cache, lf_cache):
    bsz, nq, _ = fq.shape
    npast = k_cache.shape[1]
    lf_all = jnp.concatenate([jnp.swapaxes(lf_cache, 1, 2), jnp.swapaxes(gates[..., :FOX_HEADS], 1, 2)], axis=2)
    total = npast + nq
    padded = -(-total // LANES) * LANES
    lb = next(c for c in (512, 384, 256, 128) if padded % c == 0)
    lf_all = jnp.pad(lf_all, ((0, 0), (0, 0), (0, padded - total))).reshape(bsz * FOX_HEADS, padded)
    ct = _cumsum_lanes(lf_all, lb).reshape(bsz, FOX_HEADS, padded)
    ck_new = ct[:, :, npast:total]
    return _fox_sample(fq, fk, fv, _cache_t(k_cache), _cache_t(v_cache), ck_new.reshape(bsz, FOX_HEADS * nq, 1),
                       ct[:, :, :npast], ck_new)


TM_FFN = 1024
TM_PROJ = 512
PROMPT_CHUNK = 128
SSD_BLOCK_ROWS = 512


def _block_diag_ones(n, width):
    idx = np.arange(n) // width
    return jnp.asarray(idx[:, None] == idx[None, :], BF16)


def _cols(w, splits):
    cuts = np.concatenate([[0], np.cumsum(splits)])
    return [w[:, int(cuts[i]):int(cuts[i + 1])] for i in range(len(splits))]


def _pad_cols(w, width):
    return jnp.pad(w, ((0, 0), (0, width - w.shape[1])))


def _row(v, width=None):
    v = v.reshape(1, -1).astype(F32)
    return v if width is None else _pad_cols(v, width)


def kernel(x_prompt, x_sample, p_prompt, p_sample, state_ssd_conv, state_ssd, cache_band_k, cache_band_v, cache_fox_k, cache_fox_v, cache_fox_logf, state_mlstm_C, state_mlstm_n, state_mlstm_m, norm_ffn1, ffn1_wg, ffn1_wu, ffn1_wd, norm_mix, norm_ffn2, ffn2_wg, ffn2_wu, ffn2_wd, ple_proj, ple_norm, ple_gate_norm, ple_gate_w, even_w_in, even_w_out, ssd_conv_w, ssd_conv_b, ssd_a_log, ssd_dt_bias, ssd_d, ssd_norm, band_q_norm, band_k_norm, band_rel_bias, odd_w_in, odd_w_out, fox_q_norm, fox_k_norm, fox_f_bias, mlstm_i_bias, mlstm_f_bias, mlstm_norm):
    bp, sp, _ = x_prompt.shape
    bs, ss, _ = x_sample.shape
    depth = norm_ffn1.shape[0]
    n_heads = ATT_WIDTH // HEAD_DIM
    assert ss <= CHUNK and sp % max(SSD_BLOCK_ROWS, FOX_BLK, TM_FFN) == 0 and (bs * ss) % TM_PROJ == 0

    def bf(w):
        return w.astype(BF16)

    bd = _block_diag_ones(ATT_WIDTH, HEAD_DIM)
    hp = x_prompt.reshape(bp * sp, D_MODEL)
    hs = x_sample.reshape(bs * ss, D_MODEL)
    tm_s = min(TM_FFN, bs * ss)
    even_p, even_s, odd_p, odd_s = [], [], [], []

    for i in range(depth):
        j = i // 2
        w1 = (_row(norm_ffn1[i]), bf(ffn1_wg[i]), bf(ffn1_wu[i]), bf(ffn1_wd[i]))
        hp = _ffn1(hp, *w1, tm=TM_FFN)
        hs = _ffn1(hs, *w1, tm=tm_s)
        gm = _row(norm_mix[i])

        if i % 2 == 0:
            wz, wx, wdt, wq, wk, wv = _cols(even_w_in[j], EVEN_SPLITS)
            wproj = (bf(wz), bf(wx), bf(_pad_cols(wdt, GATE_LANES)), bf(wq), bf(wk), bf(wv),
                     _row(ssd_dt_bias[j], GATE_LANES), _row(jnp.tile(band_q_norm[j], n_heads)),
                     _row(jnp.tile(band_k_norm[j], n_heads)), bd)
            ssd_w = (ssd_conv_w[j], ssd_conv_b[j], ssd_a_log[j], ssd_d[j], ssd_norm[j])

            z, xbc, dt, q, k, v = [a.reshape(bp, sp, -1) for a in _proj_even(hp, gm, *wproj, tm=TM_PROJ)]
            ya_p, conv_p, st_p = _ssd_mixer(xbc, z, dt, jnp.zeros((bp, SSD_CONV - 1, SSD_CONV_DIM), F32),
                                            jnp.zeros((bp, SSD_HEADS, HEAD_DIM, SSD_STATE), F32), *ssd_w,
                                            tb=SSD_BLOCK_ROWS, t=PROMPT_CHUNK)
            ob_p = _band_prompt(q, k, v, _band_prompt_bias(band_rel_bias[j]))
            keep = min(BAND_PAST, sp)
            even_p.append((conv_p, st_p, k[:, sp - keep:].reshape(bp, keep, n_heads, HEAD_DIM),
                           v[:, sp - keep:].reshape(bp, keep, n_heads, HEAD_DIM)))

            z, xbc, dt, q, k, v = [a.reshape(bs, ss, -1) for a in _proj_even(hs, gm, *wproj, tm=TM_PROJ)]
            ya_s, conv_s, st_s = _ssd_mixer(xbc, z, dt, state_ssd_conv[j], state_ssd[j], *ssd_w, tb=ss, t=ss)
            nb = cache_band_k.shape[2]
            ob_s = _band_sample(q, k, v, _cache_t(cache_band_k[j]), _cache_t(cache_band_v[j]),
                                *_band_sample_bias(band_rel_bias[j], ss, nb))
            even_s.append((conv_s, st_s, k.reshape(bs, ss, n_heads, HEAD_DIM), v.reshape(bs, ss, n_heads, HEAD_DIM)))

            mix_p, mix_s = (ya_p, ob_p), (ya_s, ob_s)
            w_out = even_w_out[j]
        else:
            wfq, wfk, wfv, wff, wmq, wmk, wmv, wmi, wmf, wmo = _cols(odd_w_in[j], ODD_SPLITS)
            wgt = _pad_cols(jnp.concatenate([wff, wmi, wmf], axis=1), GATE_LANES)
            gtb = _row(jnp.concatenate([fox_f_bias[j], mlstm_i_bias[j], mlstm_f_bias[j]]), GATE_LANES)
            wproj = (bf(wfq), bf(wfk), bf(wfv), bf(wgt), bf(wmq), bf(wmk), bf(wmv), bf(wmo), gtb,
                     _row(jnp.tile(fox_q_norm[j], n_heads)), _row(jnp.tile(fox_k_norm[j], n_heads)), bd)

            fq, fkt, fvt, gt, mq, mk, mv, so = _proj_odd(hp, gm, *wproj, tm=TM_PROJ, seq=sp)
            fq, gt, mq, mk, mv, so = [a.reshape(bp, sp, -1) for a in (fq, gt, mq, mk, mv, so)]
            oc_p = _fox_prompt_mixer(fq, fkt, fvt, gt)
            hm_p, c_p, n_p, m_p = _mlstm_mixer(mq, mk, mv, so, gt,
                                               jnp.zeros((bp, MLSTM_HEADS, MLSTM_HEAD_DIM, MLSTM_HEAD_DIM), F32),
                                               jnp.zeros((bp, MLSTM_HEADS, MLSTM_HEAD_DIM), F32),
                                               jnp.zeros((bp, MLSTM_HEADS), F32), mlstm_norm[j], t=PROMPT_CHUNK)
            odd_p.append((jnp.transpose(fkt.reshape(bp, n_heads, HEAD_DIM, sp), (0, 3, 1, 2)),
                          jnp.transpose(fvt.reshape(bp, n_heads, HEAD_DIM, sp), (0, 3, 1, 2)),
                          gt[..., :FOX_HEADS], c_p, n_p, m_p))

            fq, fk, fv, gt, mq, mk, mv, so = [a.reshape(bs, ss, -1) for a in _proj_odd(hs, gm, *wproj, tm=TM_PROJ)]
            oc_s = _fox_sample_mixer(fq, fk, fv, gt, cache_fox_k[j], cache_fox_v[j], cache_fox_logf[j])
            hm_s, c_s, n_s, m_s = _mlstm_mixer(mq, mk, mv, so, gt, state_mlstm_C[j], state_mlstm_n[j], state_mlstm_m[j],
                                               mlstm_norm[j], t=ss)
            odd_s.append((fk.reshape(bs, ss, n_heads, HEAD_DIM), fv.reshape(bs, ss, n_heads, HEAD_DIM),
                          gt[..., :FOX_HEADS], c_s, n_s, m_s))

            mix_p, mix_s = (oc_p, hm_p), (oc_s, hm_s)
            w_out = odd_w_out[j]

        ka = mix_p[0].shape[-1]
        wt = (bf(w_out[:ka]), bf(w_out[ka:]), _row(norm_ffn2[i]), bf(ffn2_wg[i]), bf(ffn2_wu[i]), bf(ffn2_wd[i]),
              bf(ple_proj[i]), _row(ple_norm[i]), _row(ple_gate_norm[i]), bf(ple_gate_w[i]))
        hp = _tail(hp, mix_p[0].reshape(bp * sp, -1), mix_p[1].reshape(bp * sp, -1),
                   p_prompt[i].reshape(bp * sp, PLE_DIM), *wt, tm=TM_PROJ)
        hs = _tail(hs, mix_s[0].reshape(bs * ss, -1), mix_s[1].reshape(bs * ss, -1),
                   p_sample[i].reshape(bs * ss, PLE_DIM), *wt, tm=TM_PROJ)

    def stk(states, idx):
        return jnp.stack([s[idx] for s in states])

    return (hp.reshape(bp, sp, D_MODEL), hs.reshape(bs, ss, D_MODEL),
            stk(even_p, 0), stk(even_s, 0), stk(even_p, 1), stk(even_s, 1),
            stk(even_p, 2), stk(even_s, 2), stk(even_p, 3), stk(even_s, 3),
            stk(odd_p, 0), stk(odd_s, 0), stk(odd_p, 1), stk(odd_s, 1), stk(odd_p, 2), stk(odd_s, 2),
            stk(odd_p, 3), stk(odd_s, 3), stk(odd_p, 4), stk(odd_s, 4), stk(odd_p, 5), stk(odd_s, 5))
```

```python
import functools

import numpy as np
import jax
import jax.numpy as jnp
from jax import lax
from jax.experimental import pallas as pl
from jax.experimental.pallas import tpu as pltpu

F32 = jnp.float32
BF16 = jnp.bfloat16
EPS = 1e-6
NEG_INF = -1e30

D_MODEL = 1024
D_FF = 2816
PLE_DIM = 256
HEAD_DIM = 64
CHUNK = 64
SSD_HEADS = 16
SSD_D_INNER = 1024
SSD_CONV_DIM = 1280
SSD_CONV = 4
BAND_HEADS = 8
BAND_PAST = 512
REL_CLIP = 128
FOX_HEADS = 8
MLSTM_HEADS = 4
MLSTM_HEAD_DIM = 128
ATT_WIDTH = 512
GATE_LANES = 128
EVEN_SPLITS = (1024, 1280, 16, 512, 512, 512)
ODD_SPLITS = (512, 512, 512, 8, 512, 512, 512, 4, 4, 512)

V7X_VMEM_BYTES = 64 * 1024 * 1024
VMEM_LIMIT = 56 * 1024 * 1024
FF_CHUNK = 256


def _cparams(n_axes):
    return pltpu.CompilerParams(dimension_semantics=("arbitrary",) * n_axes,
                                vmem_limit_bytes=VMEM_LIMIT)


def _resident(shape):
    nd = len(shape)
    return pl.BlockSpec(shape, lambda *_: (0,) * nd, pipeline_mode=pl.Buffered(1))


def _dot(a, b):
    return jnp.dot(a, b, preferred_element_type=F32)


def _dot_nt(a, b):
    return lax.dot_general(a, b, (((1,), (1,)), ((), ())), preferred_element_type=F32)


def _dot_tn(a, b):
    return lax.dot_general(a, b, (((0,), (0,)), ((), ())), preferred_element_type=F32)


def _split3(x):
    hi = x.astype(BF16)
    r = x - hi.astype(F32)
    mid = r.astype(BF16)
    lo = (r - mid.astype(F32)).astype(BF16)
    return hi, mid, lo


def _dot3_l(x, w01):
    hi, mid, lo = _split3(x)
    return _dot(hi, w01) + _dot(mid, w01) + _dot(lo, w01)


def _dot3_r(w01, x):
    hi, mid, lo = _split3(x)
    return _dot(w01, hi) + _dot(w01, mid) + _dot(w01, lo)


def _rms(x, g):
    ms = jnp.mean(x * x, axis=-1, keepdims=True)
    return x * lax.rsqrt(ms + EPS) * g


def _sigmoid(x):
    return 1.0 / (1.0 + jnp.exp(-x))


def _silu(x):
    return x * _sigmoid(x)


def _softplus(x):
    return jnp.maximum(x, 0.0) + jnp.log(1.0 + jnp.exp(-jnp.abs(x)))


def _log_sigmoid(x):
    return jnp.minimum(x, 0.0) - jnp.log(1.0 + jnp.exp(-jnp.abs(x)))


def _lower_tri(n):
    r = lax.broadcasted_iota(jnp.int32, (n, n), 0)
    c = lax.broadcasted_iota(jnp.int32, (n, n), 1)
    return r >= c


def _ffn_half(x, g_ref, wg_ref, wu_ref, wd_ref, acc_ref):
    xn = _rms(x, g_ref[...]).astype(BF16)
    for c in range(D_FF // FF_CHUNK):
        sl = slice(c * FF_CHUNK, (c + 1) * FF_CHUNK)
        gate = _dot(xn, wg_ref[:, sl])
        up = _dot(xn, wu_ref[:, sl])
        a = (_silu(gate) * up).astype(BF16)
        d = _dot(a, wd_ref[sl, :])
        if c == 0:
            acc_ref[...] = d
        else:
            acc_ref[...] += d
    return 0.5 * acc_ref[...]


def _ffn1_kernel(h_ref, g_ref, wg_ref, wu_ref, wd_ref, o_ref, acc_ref):
    x = h_ref[...]
    o_ref[...] = x + _ffn_half(x, g_ref, wg_ref, wu_ref, wd_ref, acc_ref)


def _ffn1(h, g, wg, wu, wd, tm):
    n = h.shape[0]
    tok = pl.BlockSpec((tm, D_MODEL), lambda i: (i, 0))
    return pl.pallas_call(
        _ffn1_kernel,
        grid=(n // tm,),
        in_specs=[tok, _resident((1, D_MODEL)), _resident((D_MODEL, D_FF)),
                  _resident((D_MODEL, D_FF)), _resident((D_FF, D_MODEL))],
        out_specs=tok,
        out_shape=jax.ShapeDtypeStruct((n, D_MODEL), F32),
        scratch_shapes=[pltpu.VMEM((tm, D_MODEL), F32)],
        compiler_params=_cparams(1),
        name="ffn1",
    )(h, g, wg, wu, wd)


def _head_rms(x, bd_ref, g, width):
    x2 = x * x
    hi = x2.astype(BF16)
    lo = (x2 - hi.astype(F32)).astype(BF16)
    ms = (_dot(hi, bd_ref[...]) + _dot(lo, bd_ref[...])) * (1.0 / width)
    return x * lax.rsqrt(ms + EPS) * g


def _proj_even_kernel(h_ref, gm_ref, wz_ref, wx_ref, wdt_ref, wq_ref, wk_ref, wv_ref,
                      dtb_ref, gq_ref, gk_ref, bd_ref,
                      z_ref, xbc_ref, dt_ref, q_ref, k_ref, v_ref):
    u = _rms(h_ref[...], gm_ref[...]).astype(BF16)
    z_ref[...] = _dot(u, wz_ref[...])
    xbc_ref[...] = _dot(u, wx_ref[...])
    dt_ref[...] = _softplus(_dot(u, wdt_ref[...]) + dtb_ref[...])
    q_ref[...] = _head_rms(_dot(u, wq_ref[...]), bd_ref, gq_ref[...], HEAD_DIM)
    k_ref[...] = _head_rms(_dot(u, wk_ref[...]), bd_ref, gk_ref[...], HEAD_DIM)
    v_ref[...] = _dot(u, wv_ref[...])


def _proj_even(h, gm, wz, wx, wdt, wq, wk, wv, dtb, gq, gk, bd, tm):
    n = h.shape[0]

    def tok(c):
        return pl.BlockSpec((tm, c), lambda i: (i, 0))

    widths = (D_MODEL, SSD_CONV_DIM, GATE_LANES, ATT_WIDTH, ATT_WIDTH, ATT_WIDTH)
    return pl.pallas_call(
        _proj_even_kernel,
        grid=(n // tm,),
        in_specs=[tok(D_MODEL), _resident((1, D_MODEL)),
                  _resident(wz.shape), _resident(wx.shape), _resident(wdt.shape),
                  _resident(wq.shape), _resident(wk.shape), _resident(wv.shape),
                  _resident(dtb.shape), _resident(gq.shape), _resident(gk.shape), _resident(bd.shape)],
        out_specs=[tok(c) for c in widths],
        out_shape=[jax.ShapeDtypeStruct((n, c), F32) for c in widths],
        compiler_params=_cparams(1),
        name="proj_even",
    )(h, gm, wz, wx, wdt, wq, wk, wv, dtb, gq, gk, bd)


def _proj_odd_kernel(h_ref, gm_ref, wfq_ref, wfk_ref, wfv_ref, wgt_ref, wmq_ref, wmk_ref, wmv_ref, wmo_ref,
                     gtb_ref, gq_ref, gk_ref, bd_ref,
                     fq_ref, fk_ref, fv_ref, gt_ref, mq_ref, mk_ref, mv_ref, so_ref, *, kv_transposed):
    u = _rms(h_ref[...], gm_ref[...]).astype(BF16)
    fq_ref[...] = _head_rms(_dot(u, wfq_ref[...]), bd_ref, gq_ref[...], HEAD_DIM)
    fk = _head_rms(_dot(u, wfk_ref[...]), bd_ref, gk_ref[...], HEAD_DIM)
    fv = _dot(u, wfv_ref[...])
    fk_ref[...] = fk.T if kv_transposed else fk
    fv_ref[...] = fv.T if kv_transposed else fv
    pre = _dot(u, wgt_ref[...]) + gtb_ref[...]
    lane = lax.broadcasted_iota(jnp.int32, pre.shape, 1)
    is_input_gate = (lane >= FOX_HEADS) & (lane < FOX_HEADS + MLSTM_HEADS)
    gt_ref[...] = jnp.where(is_input_gate, pre, _log_sigmoid(pre))
    mq_ref[...] = _dot(u, wmq_ref[...])
    mk_ref[...] = _dot(u, wmk_ref[...]) * (MLSTM_HEAD_DIM ** -0.5)
    mv_ref[...] = _dot(u, wmv_ref[...])
    so_ref[...] = _sigmoid(_dot(u, wmo_ref[...]))


def _proj_odd(h, gm, wfq, wfk, wfv, wgt, wmq, wmk, wmv, wmo, gtb, gq, gk, bd, tm, seq=None):
    n = h.shape[0]

    def tok(c):
        return pl.BlockSpec((tm, c), lambda i: (i, 0))

    widths = (ATT_WIDTH, ATT_WIDTH, ATT_WIDTH, GATE_LANES, ATT_WIDTH, ATT_WIDTH, ATT_WIDTH, ATT_WIDTH)
    out_specs = [tok(c) for c in widths]
    out_shape = [jax.ShapeDtypeStruct((n, c), F32) for c in widths]
    if seq is not None:
        per_seq = seq // tm
        for o in (1, 2):
            out_specs[o] = pl.BlockSpec((None, ATT_WIDTH, tm), lambda i: (i // per_seq, 0, i % per_seq))
            out_shape[o] = jax.ShapeDtypeStruct((n // seq, ATT_WIDTH, seq), F32)
    ws = (wfq, wfk, wfv, wgt, wmq, wmk, wmv, wmo, gtb, gq, gk, bd)
    return pl.pallas_call(
        functools.partial(_proj_odd_kernel, kv_transposed=seq is not None),
        grid=(n // tm,),
        in_specs=[tok(D_MODEL), _resident((1, D_MODEL))] + [_resident(w.shape) for w in ws],
        out_specs=out_specs,
        out_shape=out_shape,
        compiler_params=_cparams(1),
        name="proj_odd",
    )(h, gm, *ws)


def _tail_kernel(h_ref, a_ref, b_ref, p_ref, woa_ref, wob_ref, g2_ref, wg_ref, wu_ref, wd_ref,
                 wple_ref, gple_ref, ggate_ref, wgate_ref, o_ref, acc_ref):
    h = h_ref[...] + _dot(a_ref[...].astype(BF16), woa_ref[...]) + _dot(b_ref[...].astype(BF16), wob_ref[...])
    h = h + _ffn_half(h, g2_ref, wg_ref, wu_ref, wd_ref, acc_ref)
    e = _rms(_dot(p_ref[...].astype(BF16), wple_ref[...]), gple_ref[...])
    gate = _sigmoid(_dot(_rms(h, ggate_ref[...]).astype(BF16), wgate_ref[...]))
    o_ref[...] = h + e * gate


def _tail(h, a, b, p, woa, wob, g2, wg, wu, wd, wple, gple, ggate, wgate, tm):
    n = h.shape[0]

    def tok(c):
        return pl.BlockSpec((tm, c), lambda i: (i, 0))

    ws = (woa, wob, g2, wg, wu, wd, wple, gple, ggate, wgate)
    return pl.pallas_call(
        _tail_kernel,
        grid=(n // tm,),
        in_specs=[tok(D_MODEL), tok(a.shape[1]), tok(b.shape[1]), tok(PLE_DIM)] + [_resident(w.shape) for w in ws],
        out_specs=tok(D_MODEL),
        out_shape=jax.ShapeDtypeStruct((n, D_MODEL), F32),
        scratch_shapes=[pltpu.VMEM((tm, D_MODEL), F32)],
        compiler_params=_cparams(1),
        name="tail",
    )(h, a, b, p, *ws)


CONV_PAD = 8
SSD_PAIR = 128
SSD_GROUP_COLS = 512
SSD_STATE = 64


def _ssd_kernel(xbc_ref, z_ref, dt_ref, dtT_ref, conv0_ref, st0_ref, cw_ref, cb_ref,
                alog_ref, alogT_ref, dskip_ref, gn_ref, e_ref,
                y_ref, convo_ref, sto_ref, xp_ref, st_ref, *, tb, t):
    i = pl.program_id(1)

    @pl.when(i == 0)
    def _():
        xp_ref[0:CONV_PAD, :] = conv0_ref[...]
        st_ref[...] = jnp.zeros_like(st_ref)
        st_ref[0:SSD_STATE, 0:SSD_GROUP_COLS] = st0_ref[0]
        st_ref[SSD_STATE:2 * SSD_STATE, SSD_GROUP_COLS:2 * SSD_GROUP_COLS] = st0_ref[1]

    xp_ref[CONV_PAD:CONV_PAD + tb, :] = xbc_ref[...]
    cw = cw_ref[...]
    pre = cb_ref[...]
    for j in range(SSD_CONV):
        off = CONV_PAD - (SSD_CONV - 1) + j
        pre = pre + cw[j:j + 1, :] * xp_ref[off:off + tb, :]
    xc = _silu(pre)
    tail_rows = xp_ref[tb:tb + CONV_PAD, :]
    convo_ref[...] = tail_rows
    xp_ref[0:CONV_PAD, :] = tail_rows

    tri = _lower_tri(t)
    ltri = tri.astype(BF16)
    utri = (lax.broadcasted_iota(jnp.int32, (t, t), 0) <= lax.broadcasted_iota(jnp.int32, (t, t), 1)).astype(BF16)
    a_row = -jnp.exp(alog_ref[...])
    a_col = -jnp.exp(alogT_ref[...])
    e01 = e_ref[...]
    lane = lax.broadcasted_iota(jnp.int32, (t, SSD_PAIR), 1)
    st_r = lax.broadcasted_iota(jnp.int32, st_ref.shape, 0)
    st_c = lax.broadcasted_iota(jnp.int32, st_ref.shape, 1)
    st_own = (st_r < SSD_STATE) == (st_c < SSD_GROUP_COLS)

    for c in range(tb // t):
        r0 = c * t
        xs = xc[r0:r0 + t, 0:SSD_D_INNER]
        b2 = xc[r0:r0 + t, SSD_D_INNER:SSD_D_INNER + 128]
        c2 = xc[r0:r0 + t, SSD_D_INNER + 128:SSD_D_INNER + 256]
        dtc = dt_ref[r0:r0 + t, :]
        acum = _dot3_r(ltri, dtc * a_row)
        acum_t = _dot3_l(dtT_ref[:, r0:r0 + t] * a_col, utri)
        acum_x = _dot3_l(acum, e01)
        xdt = xs * _dot3_l(dtc, e01)
        last = acum_x[t - 1:t, :]
        tailx = jnp.exp(last - acum_x) * xdt
        b2b = b2.astype(BF16)
        y = _dot(c2.astype(BF16), st_ref[...].astype(BF16)) * jnp.exp(acum_x) + dskip_ref[...] * xs
        cbs = []
        for g in range(2):
            in_g = (lane >= HEAD_DIM * g) & (lane < HEAD_DIM * (g + 1))
            cbs.append(_dot_nt(jnp.where(in_g, c2, 0.0).astype(BF16), b2b))
        ys = []
        for j in range(SSD_HEADS // 2):
            xpair = xdt[:, SSD_PAIR * j:SSD_PAIR * (j + 1)]
            acc = None
            for s in range(2):
                hd = 2 * j + s
                seg = acum[:, hd:hd + 1] - acum_t[hd:hd + 1, :]
                m = jnp.exp(jnp.where(tri, seg, NEG_INF)) * cbs[j // 4]
                xh = jnp.where((lane < HEAD_DIM) if s == 0 else (lane >= HEAD_DIM), xpair, 0.0)
                d = _dot(m.astype(BF16), xh.astype(BF16))
                acc = d if acc is None else acc + d
            ys.append(acc)
        y = y + jnp.concatenate(ys, axis=1)
        upd = _dot_tn(b2b, tailx.astype(BF16))
        st_ref[...] = jnp.where(st_own, st_ref[...] * jnp.exp(last) + upd, 0.0)
        y_ref[r0:r0 + t, :] = _rms(y * _silu(z_ref[r0:r0 + t, :]), gn_ref[...])

    sto_ref[0] = st_ref[0:SSD_STATE, 0:SSD_GROUP_COLS]
    sto_ref[1] = st_ref[SSD_STATE:2 * SSD_STATE, SSD_GROUP_COLS:2 * SSD_GROUP_COLS]


def _ssd(xbc, z, dt, dt_t, conv0, st0, cw, cb, alog, alog_t, dskip, gn, e01, tb, t):
    bsz, n, _ = xbc.shape
    consts = (cw, cb, alog, alog_t, dskip, gn, e01)
    return pl.pallas_call(
        functools.partial(_ssd_kernel, tb=tb, t=t),
        grid=(bsz, n // tb),
        in_specs=[pl.BlockSpec((None, tb, SSD_CONV_DIM), lambda b, i: (b, i, 0)),
                  pl.BlockSpec((None, tb, SSD_D_INNER), lambda b, i: (b, i, 0)),
                  pl.BlockSpec((None, tb, GATE_LANES), lambda b, i: (b, i, 0)),
                  pl.BlockSpec((None, SSD_HEADS, tb), lambda b, i: (b, 0, i)),
                  pl.BlockSpec((None, CONV_PAD, SSD_CONV_DIM), lambda b, i: (b, 0, 0)),
                  pl.BlockSpec((None, 2, SSD_STATE, SSD_GROUP_COLS), lambda b, i: (b, 0, 0, 0))]
                 + [_resident(w.shape) for w in consts],
        out_specs=[pl.BlockSpec((None, tb, SSD_D_INNER), lambda b, i: (b, i, 0)),
                   pl.BlockSpec((None, CONV_PAD, SSD_CONV_DIM), lambda b, i: (b, 0, 0)),
                   pl.BlockSpec((None, 2, SSD_STATE, SSD_GROUP_COLS), lambda b, i: (b, 0, 0, 0))],
        out_shape=[jax.ShapeDtypeStruct((bsz, n, SSD_D_INNER), F32),
                   jax.ShapeDtypeStruct((bsz, CONV_PAD, SSD_CONV_DIM), F32),
                   jax.ShapeDtypeStruct((bsz, 2, SSD_STATE, SSD_GROUP_COLS), F32)],
        scratch_shapes=[pltpu.VMEM((CONV_PAD + tb, SSD_CONV_DIM), F32),
                        pltpu.VMEM((2 * SSD_STATE, 2 * SSD_GROUP_COLS), F32)],
        compiler_params=_cparams(2),
        name="ssd",
    )(xbc, z, dt, dt_t, conv0, st0, *consts)


def _head_expand(n_heads, width):
    e = np.zeros((GATE_LANES, n_heads * width), np.float32)
    for h in range(n_heads):
        e[h, h * width:(h + 1) * width] = 1.0
    return jnp.asarray(e, BF16)


def _ssd_mixer(xbc, z, dt, conv_buf, s0, conv_w, conv_b, a_log, d_skip, ssd_norm, tb, t):
    bsz = xbc.shape[0]
    dt_t = jnp.swapaxes(dt[..., :SSD_HEADS], 1, 2)
    conv0 = jnp.pad(conv_buf, ((0, 0), (CONV_PAD - (SSD_CONV - 1), 0), (0, 0)))
    st0 = s0.reshape(bsz, 2, 8, HEAD_DIM, SSD_STATE).transpose(0, 1, 4, 2, 3).reshape(bsz, 2, SSD_STATE, SSD_GROUP_COLS)
    cw = jnp.pad(conv_w, ((0, CONV_PAD - SSD_CONV), (0, 0)))
    alog = jnp.pad(a_log, (0, GATE_LANES - SSD_HEADS)).reshape(1, GATE_LANES)
    y, convo, sto = _ssd(xbc, z, dt, dt_t, conv0, st0, cw, conv_b.reshape(1, -1), alog, a_log.reshape(SSD_HEADS, 1),
                         jnp.repeat(d_skip, HEAD_DIM).reshape(1, -1), ssd_norm.reshape(1, -1),
                         _head_expand(SSD_HEADS, HEAD_DIM), tb, t)
    s_new = sto.reshape(bsz, 2, SSD_STATE, 8, HEAD_DIM).transpose(0, 1, 3, 4, 2).reshape(bsz, SSD_HEADS, HEAD_DIM, SSD_STATE)
    return y, convo[:, CONV_PAD - (SSD_CONV - 1):], s_new


GATE_I = FOX_HEADS
GATE_F = FOX_HEADS + MLSTM_HEADS


def _mlstm_kernel(q_ref, k_ref, v_ref, so_ref, gt_ref, gtT_ref, c0_ref, n0_ref, m0_ref, gn_ref,
                  h_ref, co_ref, no_ref, mo_ref, c_ref, n_ref, m_ref, *, t):
    i = pl.program_id(1)

    @pl.when(i == 0)
    def _():
        c_ref[...] = c0_ref[...]
        n_ref[...] = n0_ref[...]
        m_ref[...] = m0_ref[...]

    tri = _lower_tri(t)
    ltri = tri.astype(BF16)
    utri = (lax.broadcasted_iota(jnp.int32, (t, t), 0) <= lax.broadcasted_iota(jnp.int32, (t, t), 1)).astype(BF16)
    gt = gt_ref[...]
    gt_t = gtT_ref[...]
    cum = _dot3_r(ltri, gt)
    cum_t = _dot3_l(gt_t, utri)
    m_vec = m_ref[...]
    m_lane = lax.broadcasted_iota(jnp.int32, m_vec.shape, 1)
    m_next = m_vec
    for hh in range(MLSTM_HEADS):
        sl = slice(MLSTM_HEAD_DIM * hh, MLSTM_HEAD_DIM * (hh + 1))
        bcol = cum[:, GATE_F + hh:GATE_F + hh + 1]
        brow = cum_t[GATE_F + hh:GATE_F + hh + 1, :]
        ig_row = gt_t[GATE_I + hh:GATE_I + hh + 1, :]
        ig_col = gt[:, GATE_I + hh:GATE_I + hh + 1]
        dmat = jnp.where(tri, bcol - brow + ig_row, NEG_INF)
        g = bcol + m_vec[:, hh:hh + 1]
        m_t = jnp.maximum(g, jnp.max(dmat, axis=-1, keepdims=True))
        w = jnp.exp(dmat - m_t)
        inter = jnp.exp(g - m_t)
        qh = q_ref[:, sl]
        kh = k_ref[:, sl]
        vh = v_ref[:, sl]
        qb = qh.astype(BF16)
        kb = kh.astype(BF16)
        a = w * _dot_nt(qb, kb)
        cm = c_ref[hh]
        nv = n_ref[hh:hh + 1, :]
        num = inter * _dot(qb, cm.astype(BF16)) + _dot(a.astype(BF16), vh.astype(BF16))
        den = inter * jnp.sum(qh * nv, axis=-1, keepdims=True) + jnp.sum(a, axis=-1, keepdims=True)
        hout = num / jnp.maximum(jnp.abs(den), jnp.exp(-m_t))
        m_last = m_t[t - 1:t, :]
        inter_last = inter[t - 1:t, :]
        w_end = jnp.exp(bcol[t - 1:t, :] - bcol + ig_col - m_last)
        c_ref[hh] = inter_last * cm + _dot_tn(kb, (w_end * vh).astype(BF16))
        n_ref[hh:hh + 1, :] = inter_last * nv + jnp.sum(w_end * kh, axis=0, keepdims=True)
        m_next = jnp.where(m_lane == hh, m_last, m_next)
        h_ref[:, sl] = _rms(hout, gn_ref[:, sl]) * so_ref[:, sl]
    m_ref[...] = m_next
    co_ref[...] = c_ref[...]
    no_ref[...] = n_ref[...]
    mo_ref[...] = m_next


def _mlstm(q, k, v, so, gt, gt_t, c0, n0, m0, gn, t):
    bsz, n, _ = q.shape
    tok = pl.BlockSpec((None, t, ATT_WIDTH), lambda b, i: (b, i, 0))
    c_spec = pl.BlockSpec((None, MLSTM_HEADS, MLSTM_HEAD_DIM, MLSTM_HEAD_DIM), lambda b, i: (b, 0, 0, 0))
    n_spec = pl.BlockSpec((None, MLSTM_HEADS, MLSTM_HEAD_DIM), lambda b, i: (b, 0, 0))
    m_spec = pl.BlockSpec((None, 1, GATE_LANES), lambda b, i: (b, 0, 0))
    return pl.pallas_call(
        functools.partial(_mlstm_kernel, t=t),
        grid=(bsz, n // t),
        in_specs=[tok, tok, tok, tok,
                  pl.BlockSpec((None, t, GATE_LANES), lambda b, i: (b, i, 0)),
                  pl.BlockSpec((None, 16, t), lambda b, i: (b, 0, i)),
                  c_spec, n_spec, m_spec, _resident(gn.shape)],
        out_specs=[tok, c_spec, n_spec, m_spec],
        out_shape=[jax.ShapeDtypeStruct((bsz, n, ATT_WIDTH), F32),
                   jax.ShapeDtypeStruct(c0.shape, F32),
                   jax.ShapeDtypeStruct(n0.shape, F32),
                   jax.ShapeDtypeStruct(m0.shape, F32)],
        scratch_shapes=[pltpu.VMEM((MLSTM_HEADS, MLSTM_HEAD_DIM, MLSTM_HEAD_DIM), F32),
                        pltpu.VMEM((MLSTM_HEADS, MLSTM_HEAD_DIM), F32),
                        pltpu.VMEM((1, GATE_LANES), F32)],
        compiler_params=_cparams(2),
        name="mlstm",
    )(q, k, v, so, gt, gt_t, c0, n0, m0, gn)


def _mlstm_mixer(mq, mk, mv, so, gates, c0, n0, m0, ml_norm, t):
    gt_t = jnp.swapaxes(gates[..., :16], 1, 2)
    m0p = jnp.pad(m0, ((0, 0), (0, GATE_LANES - MLSTM_HEADS)))[:, None, :]
    h, c_new, n_new, m_new = _mlstm(mq, mk, mv, so, gates, gt_t, c0, n0, m0p, ml_norm.reshape(1, -1), t)
    return h, c_new, n_new, m_new[:, 0, :MLSTM_HEADS]


PAIR = 2 * HEAD_DIM
N_PAIRS = ATT_WIDTH // PAIR
QK_SCALE = HEAD_DIM ** -0.5


def _in_head(lane, h):
    return (lane >= HEAD_DIM * h) & (lane < HEAD_DIM * (h + 1))


def _stack_heads(q):
    lane = lax.broadcasted_iota(jnp.int32, q.shape, 1)
    return jnp.concatenate([jnp.where(_in_head(lane, h), q, 0.0) for h in range(ATT_WIDTH // HEAD_DIM)], axis=0)


def _unstack_heads(o, nq):
    lane = lax.broadcasted_iota(jnp.int32, (nq, ATT_WIDTH), 1)
    out = o[0:nq, :]
    for h in range(1, ATT_WIDTH // HEAD_DIM):
        out = jnp.where(_in_head(lane, h), o[h * nq:(h + 1) * nq, :], out)
    return out


def _rows_per_head(x_t, nq):
    n = x_t.shape[1]
    return jnp.concatenate([jnp.broadcast_to(x_t[h:h + 1, :], (nq, n)) for h in range(x_t.shape[0])], axis=0)


BAND_QB = BAND_PAST // 2
BAND_STEP_HEADS = 4


def _band_prompt_kernel(q_ref, k2_ref, k1_ref, k0_ref, v2_ref, v1_ref, v0_ref, bias_ref, o_ref):
    i = pl.program_id(2)
    qb, width = q_ref.shape
    lane = lax.broadcasted_iota(jnp.int32, (qb, width), 1)
    q = q_ref[...] * QK_SCALE
    kcat = jnp.concatenate([k2_ref[...], k1_ref[...], k0_ref[...]], axis=0).astype(BF16)
    vcat = jnp.concatenate([v2_ref[...], v1_ref[...], v0_ref[...]], axis=0).astype(BF16)
    col = lax.broadcasted_iota(jnp.int32, (qb, 3 * qb), 1)
    in_sequence = col >= (2 - i) * qb
    out = None
    for s in range(width // HEAD_DIM):
        qh = jnp.where(_in_head(lane, s), q, 0.0).astype(BF16)
        sc = jnp.where(in_sequence, _dot_nt(qh, kcat) + bias_ref[s], NEG_INF)
        p = jnp.exp(sc - jnp.max(sc, axis=-1, keepdims=True))
        o = _dot(p.astype(BF16), vcat) / jnp.sum(p, axis=-1, keepdims=True)
        out = o if out is None else jnp.where(_in_head(lane, s), o, out)
    o_ref[...] = out


def _band_prompt(q, k, v, bias):
    bsz, n, _ = q.shape
    qb = BAND_QB

    width = BAND_STEP_HEADS * HEAD_DIM

    def kv_spec(back):
        return pl.BlockSpec((None, qb, width), lambda b, j, i: (b, jnp.maximum(i - back, 0), j))

    return pl.pallas_call(
        _band_prompt_kernel,
        grid=(bsz, ATT_WIDTH // width, n // qb),
        in_specs=[kv_spec(0), kv_spec(2), kv_spec(1), kv_spec(0), kv_spec(2), kv_spec(1), kv_spec(0),
                  pl.BlockSpec((BAND_STEP_HEADS, qb, 3 * qb), lambda b, j, i: (j, 0, 0))],
        out_specs=kv_spec(0),
        out_shape=jax.ShapeDtypeStruct((bsz, n, ATT_WIDTH), F32),
        compiler_params=_cparams(3),
        name="band_prompt",
    )(q, k, k, k, v, v, v, bias)


def _rel_bias_block(rel_table, rows, cols, offset):
    period = rows + cols - 1
    dist = offset + rows - 1 - np.arange(period)
    profile = rel_table[:, np.clip(dist, -REL_CLIP, REL_CLIP) + REL_CLIP].astype(F32)
    rolled = jnp.roll(profile, -(rows - 1), axis=1)
    flat = jnp.tile(rolled, (1, rows))[:, :rows * (period - 1)]
    return flat.reshape(-1, rows, period - 1)[:, :, :cols]


def _band_prompt_bias(rel_table):
    qb = BAND_QB
    r = np.arange(qb)[:, None] + 2 * qb
    c = np.arange(3 * qb)[None, :]
    in_band = (c // CHUNK <= r // CHUNK) & (c // CHUNK >= r // CHUNK - BAND_PAST // CHUNK)
    return jnp.where(jnp.asarray(in_band)[None], _rel_bias_block(rel_table, qb, 3 * qb, 2 * qb), NEG_INF)


def _band_sample_kernel(q_ref, kct_ref, vct_ref, kn_ref, vn_ref, bc_ref, bn_ref, o_ref):
    nq = q_ref.shape[0]
    qs = _stack_heads(q_ref[...] * QK_SCALE).astype(BF16)
    s_c = _dot(qs, kct_ref[...].astype(BF16)) + bc_ref[...]
    s_n = _dot_nt(qs, kn_ref[...].astype(BF16)) + bn_ref[...]
    m = jnp.maximum(jnp.max(s_c, axis=-1, keepdims=True), jnp.max(s_n, axis=-1, keepdims=True))
    p_c = jnp.exp(s_c - m)
    p_n = jnp.exp(s_n - m)
    l = jnp.sum(p_c, axis=-1, keepdims=True) + jnp.sum(p_n, axis=-1, keepdims=True)
    o = _dot_nt(p_c.astype(BF16), vct_ref[...].astype(BF16)) + _dot(p_n.astype(BF16), vn_ref[...].astype(BF16))
    o_ref[...] = _unstack_heads(o / l, nq)


def _band_sample(q, k, v, kt_cache, vt_cache, bias_c, bias_n):
    bsz, nq, _ = q.shape
    nb = kt_cache.shape[2]
    new = pl.BlockSpec((None, nq, ATT_WIDTH), lambda b: (b, 0, 0))
    old = pl.BlockSpec((None, ATT_WIDTH, nb), lambda b: (b, 0, 0))
    return pl.pallas_call(
        _band_sample_kernel,
        grid=(bsz,),
        in_specs=[new, old, old, new, new, _resident(bias_c.shape), _resident(bias_n.shape)],
        out_specs=new,
        out_shape=jax.ShapeDtypeStruct((bsz, nq, ATT_WIDTH), F32),
        compiler_params=_cparams(1),
        name="band_sample",
    )(q, kt_cache, vt_cache, k, v, bias_c, bias_n)


def _band_sample_bias(rel_table, nq, nb):
    bias = _rel_bias_block(rel_table, nq, nb + nq, nb).reshape(BAND_HEADS * nq, nb + nq)
    return bias[:, :nb], bias[:, nb:]


def _cumsum_lanes_kernel(x_ref, o_ref, carry_ref):
    lb = x_ref.shape[1]

    @pl.when(pl.program_id(0) == 0)
    def _():
        carry_ref[...] = jnp.zeros_like(carry_ref)

    utri = (lax.broadcasted_iota(jnp.int32, (lb, lb), 0) <= lax.broadcasted_iota(jnp.int32, (lb, lb), 1)).astype(BF16)
    c = _dot3_l(x_ref[...], utri) + carry_ref[...]
    o_ref[...] = c
    carry_ref[...] = c[:, lb - 1:lb]


def _cumsum_lanes(x, lb):
    rows, n = x.shape
    blk = pl.BlockSpec((rows, lb), lambda i: (0, i))
    return pl.pallas_call(
        _cumsum_lanes_kernel,
        grid=(n // lb,),
        in_specs=[blk],
        out_specs=blk,
        out_shape=jax.ShapeDtypeStruct((rows, n), F32),
        scratch_shapes=[pltpu.VMEM((rows, 1), F32)],
        compiler_params=_cparams(1),
        name="cumsum_lanes",
    )(x)


FOX_BLK = 512
FOX_UNROLL = 4


LOG2E = 1.4426950408889634


def _fox_prompt_kernel(q_ref, kt_ref, vt_ref, cq_ref, ck_ref, o_ref, m_ref, acc_ref):
    i = pl.program_id(2)
    blk = q_ref.shape[0]
    lane = lax.broadcasted_iota(jnp.int32, (blk, PAIR), 1)
    q = q_ref[...] * (QK_SCALE * LOG2E)
    q2 = jnp.concatenate([jnp.where(_in_head(lane, s), q, 0.0) for s in range(2)], axis=0).astype(BF16)
    cq = cq_ref[...] * LOG2E
    cqb = jnp.concatenate([jnp.broadcast_to(cq[:, s:s + 1], (blk, PAIR)) for s in range(2)], axis=0)
    tri = _lower_tri(blk)
    tri2 = jnp.concatenate([tri, tri], axis=0)
    ones = jnp.ones((PAIR, blk), BF16)
    m_ref[...] = jnp.full_like(m_ref, NEG_INF)
    acc_ref[...] = jnp.zeros_like(acc_ref)

    def step(kk, on_diagonal):
        k0 = pl.multiple_of(kk * blk, blk)
        kb = kt_ref[:, pl.ds(k0, blk)].astype(BF16)
        vb = jnp.concatenate([vt_ref[:, pl.ds(k0, blk)].astype(BF16), ones], axis=0)
        ck = ck_ref[:, pl.ds(k0, blk)] * LOG2E
        sc = _dot(q2, kb)
        sc = jnp.concatenate([sc[:blk] - ck[0:1], sc[blk:] - ck[1:2]], axis=0)
        if on_diagonal:
            sc = jnp.where(tri2, sc, NEG_INF)
        chunks = [sc[:, PAIR * c:PAIR * (c + 1)] for c in range(blk // PAIR)]
        rm = chunks[0]
        for ch in chunks[1:]:
            rm = jnp.maximum(rm, ch)
        rmb = jnp.broadcast_to(jnp.max(rm, axis=-1, keepdims=True), (2 * blk, PAIR))
        m_old = m_ref[...]
        m_new = jnp.maximum(m_old, rmb + cqb)
        shift = m_new - cqb
        p = jnp.concatenate([jnp.exp2(ch - shift) for ch in chunks], axis=1).astype(BF16)
        alpha = jnp.exp2(m_old - m_new)
        acc_ref[...] = jnp.concatenate([alpha, alpha], axis=1) * acc_ref[...] + _dot_nt(p, vb)
        m_ref[...] = m_new

    def body(quad, carry):
        for u in range(FOX_UNROLL):
            step(FOX_UNROLL * quad + u, False)
        return carry

    lax.fori_loop(0, lax.shift_right_logical(i, FOX_UNROLL.bit_length() - 1), body, 0)
    done = jnp.bitwise_and(i, -FOX_UNROLL)
    width = FOX_UNROLL // 2
    while width >= 1:
        @pl.when(jnp.bitwise_and(i, width) != 0)
        def _(done=done, width=width):
            for u in range(width):
                step(done + u, False)
        done = done + jnp.bitwise_and(i, width)
        width //= 2

    step(i, True)
    acc = acc_ref[...]
    o = acc[:, :PAIR] / acc[:, PAIR:]
    o_ref[...] = jnp.where(lane < HEAD_DIM, o[:blk], o[blk:])


def _fox_prompt(q, kt, vt, ct_col, ct_row):
    bsz, n, _ = q.shape
    blk = FOX_BLK
    tile = pl.BlockSpec((None, blk, PAIR), lambda b, j, i: (b, i, j))
    whole = pl.BlockSpec((None, PAIR, n), lambda b, j, i: (b, j, 0))
    return pl.pallas_call(
        _fox_prompt_kernel,
        grid=(bsz, N_PAIRS, n // blk),
        in_specs=[tile, whole, whole,
                  pl.BlockSpec((None, None, blk, 2), lambda b, j, i: (b, j, i, 0)),
                  pl.BlockSpec((None, None, 2, n), lambda b, j, i: (b, j, 0, 0))],
        out_specs=tile,
        out_shape=jax.ShapeDtypeStruct((bsz, n, ATT_WIDTH), F32),
        scratch_shapes=[pltpu.VMEM((2 * blk, PAIR), F32), pltpu.VMEM((2 * blk, 2 * PAIR), F32)],
        compiler_params=_cparams(3),
        name="fox_prompt",
    )(q, kt, vt, ct_col, ct_row)


FOX_KV_BLK = 2048


def _fox_sample_kernel(q_ref, kct_ref, vct_ref, kn_ref, vn_ref, cq_ref, ckc_ref, ckn_ref, o_ref, m_ref, l_ref, acc_ref):
    j = pl.program_id(1)
    nq = q_ref.shape[0]

    @pl.when(j == 0)
    def _():
        m_ref[...] = jnp.full_like(m_ref, NEG_INF)
        l_ref[...] = jnp.zeros_like(l_ref)
        acc_ref[...] = jnp.zeros_like(acc_ref)

    qs = _stack_heads(q_ref[...] * QK_SCALE).astype(BF16)
    cq = cq_ref[...]

    def absorb(sc, pv):
        m_old = m_ref[...]
        m_new = jnp.maximum(m_old, jnp.max(sc, axis=-1, keepdims=True))
        alpha = jnp.exp(m_old - m_new)
        p = jnp.exp(sc - m_new)
        l_ref[...] = alpha * l_ref[...] + jnp.sum(p, axis=-1, keepdims=True)
        acc_ref[...] = alpha * acc_ref[...] + pv(p.astype(BF16))
        m_ref[...] = m_new

    vct = vct_ref[...].astype(BF16)
    absorb(_dot(qs, kct_ref[...].astype(BF16)) + (cq - _rows_per_head(ckc_ref[...], nq)), lambda p: _dot_nt(p, vct))

    @pl.when(j == pl.num_programs(1) - 1)
    def _():
        sc = _dot_nt(qs, kn_ref[...].astype(BF16)) + (cq - _rows_per_head(ckn_ref[...], nq))
        row = lax.broadcasted_iota(jnp.int32, sc.shape, 0)
        col = lax.broadcasted_iota(jnp.int32, sc.shape, 1)
        vn = vn_ref[...].astype(BF16)
        absorb(jnp.where(col <= row % nq, sc, NEG_INF), lambda p: _dot(p, vn))
        o_ref[...] = _unstack_heads(acc_ref[...] / l_ref[...], nq)


def _fox_sample(q, k, v, kt_cache, vt_cache, cq, ck_cache, ck_new):
    bsz, nq, _ = q.shape
    npast = kt_cache.shape[2]
    kvb = FOX_KV_BLK
    new = pl.BlockSpec((None, nq, ATT_WIDTH), lambda b, j: (b, 0, 0))
    old = pl.BlockSpec((None, ATT_WIDTH, kvb), lambda b, j: (b, 0, j))
    rows = FOX_HEADS * nq
    return pl.pallas_call(
        _fox_sample_kernel,
        grid=(bsz, npast // kvb),
        in_specs=[new, old, old, new, new,
                  pl.BlockSpec((None, rows, 1), lambda b, j: (b, 0, 0)),
                  pl.BlockSpec((None, FOX_HEADS, kvb), lambda b, j: (b, 0, j)),
                  pl.BlockSpec((None, FOX_HEADS, nq), lambda b, j: (b, 0, 0))],
        out_specs=new,
        out_shape=jax.ShapeDtypeStruct((bsz, nq, ATT_WIDTH), F32),
        scratch_shapes=[pltpu.VMEM((rows, 1), F32), pltpu.VMEM((rows, 1), F32), pltpu.VMEM((rows, ATT_WIDTH), F32)],
        compiler_params=_cparams(2),
        name="fox_sample",
    )(q, kt_cache, vt_cache, k, v, cq, ck_cache, ck_new)


LANES = 128


def _fox_prompt_mixer(fq, fk, fv, gates):
    bsz, n, _ = fq.shape
    logf_t = jnp.swapaxes(gates[..., :FOX_HEADS], 1, 2).reshape(bsz * FOX_HEADS, n)
    ct = _cumsum_lanes(logf_t, FOX_BLK).reshape(bsz, N_PAIRS, 2, n)
    return _fox_prompt(fq, fk, fv, jnp.swapaxes(ct, 2, 3), ct)


def _cache_t(cache):
    bsz, npast = cache.shape[:2]
    return jnp.transpose(cache, (0, 2, 3, 1)).reshape(bsz, ATT_WIDTH, npast)


def _fox_sample_mixer(fq, fk, fv, gates, k_cache, v_cache, lf_cache):
    bsz, nq, _ = fq.shape
    npast = k_cache.shape[1]
    lf_all = jnp.concatenate([jnp.swapaxes(lf_cache, 1, 2), jnp.swapaxes(gates[..., :FOX_HEADS], 1, 2)], axis=2)
    total = npast + nq
    padded = -(-total // LANES) * LANES
    lb = next(c for c in (512, 384, 256, 128) if padded % c == 0)
    lf_all = jnp.pad(lf_all, ((0, 0), (0, 0), (0, padded - total))).reshape(bsz * FOX_HEADS, padded)
    ct = _cumsum_lanes(lf_all, lb).reshape(bsz, FOX_HEADS, padded)
    ck_new = ct[:, :, npast:total]
    return _fox_sample(fq, fk, fv, _cache_t(k_cache), _cache_t(v_cache), ck_new.reshape(bsz, FOX_HEADS * nq, 1),
                       ct[:, :, :npast], ck_new)


TM_FFN = 1024
TM_PROJ = 512
PROMPT_CHUNK = 128
SSD_BLOCK_ROWS = 512


def _block_diag_ones(n, width):
    idx = np.arange(n) // width
    return jnp.asarray(idx[:, None] == idx[None, :], BF16)


def _cols(w, splits):
    cuts = np.concatenate([[0], np.cumsum(splits)])
    return [w[:, int(cuts[i]):int(cuts[i + 1])] for i in range(len(splits))]


def _pad_cols(w, width):
    return jnp.pad(w, ((0, 0), (0, width - w.shape[1])))


def _row(v, width=None):
    v = v.reshape(1, -1).astype(F32)
    return v if width is None else _pad_cols(v, width)


def kernel(x_prompt, x_sample, p_prompt, p_sample, state_ssd_conv, state_ssd, cache_band_k, cache_band_v, cache_fox_k, cache_fox_v, cache_fox_logf, state_mlstm_C, state_mlstm_n, state_mlstm_m, norm_ffn1, ffn1_wg, ffn1_wu, ffn1_wd, norm_mix, norm_ffn2, ffn2_wg, ffn2_wu, ffn2_wd, ple_proj, ple_norm, ple_gate_norm, ple_gate_w, even_w_in, even_w_out, ssd_conv_w, ssd_conv_b, ssd_a_log, ssd_dt_bias, ssd_d, ssd_norm, band_q_norm, band_k_norm, band_rel_bias, odd_w_in, odd_w_out, fox_q_norm, fox_k_norm, fox_f_bias, mlstm_i_bias, mlstm_f_bias, mlstm_norm):
    bp, sp, _ = x_prompt.shape
    bs, ss, _ = x_sample.shape
    depth = norm_ffn1.shape[0]
    n_heads = ATT_WIDTH // HEAD_DIM
    assert ss <= CHUNK and sp % max(SSD_BLOCK_ROWS, FOX_BLK, TM_FFN) == 0 and (bs * ss) % TM_PROJ == 0

    def bf(w):
        return w.astype(BF16)

    bd = _block_diag_ones(ATT_WIDTH, HEAD_DIM)
    hp = x_prompt.reshape(bp * sp, D_MODEL)
    hs = x_sample.reshape(bs * ss, D_MODEL)
    tm_s = min(TM_FFN, bs * ss)
    even_p, even_s, odd_p, odd_s = [], [], [], []

    for i in range(depth):
        j = i // 2
        w1 = (_row(norm_ffn1[i]), bf(ffn1_wg[i]), bf(ffn1_wu[i]), bf(ffn1_wd[i]))
        hp = _ffn1(hp, *w1, tm=TM_FFN)
        hs = _ffn1(hs, *w1, tm=tm_s)
        gm = _row(norm_mix[i])

        if i % 2 == 0:
            wz, wx, wdt, wq, wk, wv = _cols(even_w_in[j], EVEN_SPLITS)
            wproj = (bf(wz), bf(wx), bf(_pad_cols(wdt, GATE_LANES)), bf(wq), bf(wk), bf(wv),
                     _row(ssd_dt_bias[j], GATE_LANES), _row(jnp.tile(band_q_norm[j], n_heads)),
                     _row(jnp.tile(band_k_norm[j], n_heads)), bd)
            ssd_w = (ssd_conv_w[j], ssd_conv_b[j], ssd_a_log[j], ssd_d[j], ssd_norm[j])

            z, xbc, dt, q, k, v = [a.reshape(bp, sp, -1) for a in _proj_even(hp, gm, *wproj, tm=TM_PROJ)]
            ya_p, conv_p, st_p = _ssd_mixer(xbc, z, dt, jnp.zeros((bp, SSD_CONV - 1, SSD_CONV_DIM), F32),
                                            jnp.zeros((bp, SSD_HEADS, HEAD_DIM, SSD_STATE), F32), *ssd_w,
                                            tb=SSD_BLOCK_ROWS, t=PROMPT_CHUNK)
            ob_p = _band_prompt(q, k, v, _band_prompt_bias(band_rel_bias[j]))
            keep = min(BAND_PAST, sp)
            even_p.append((conv_p, st_p, k[:, sp - keep:].reshape(bp, keep, n_heads, HEAD_DIM),
                           v[:, sp - keep:].reshape(bp, keep, n_heads, HEAD_DIM)))

            z, xbc, dt, q, k, v = [a.reshape(bs, ss, -1) for a in _proj_even(hs, gm, *wproj, tm=TM_PROJ)]
            ya_s, conv_s, st_s = _ssd_mixer(xbc, z, dt, state_ssd_conv[j], state_ssd[j], *ssd_w, tb=ss, t=ss)
            nb = cache_band_k.shape[2]
            ob_s = _band_sample(q, k, v, _cache_t(cache_band_k[j]), _cache_t(cache_band_v[j]),
                                *_band_sample_bias(band_rel_bias[j], ss, nb))
            even_s.append((conv_s, st_s, k.reshape(bs, ss, n_heads, HEAD_DIM), v.reshape(bs, ss, n_heads, HEAD_DIM)))

            mix_p, mix_s = (ya_p, ob_p), (ya_s, ob_s)
            w_out = even_w_out[j]
        else:
            wfq, wfk, wfv, wff, wmq, wmk, wmv, wmi, wmf, wmo = _cols(odd_w_in[j], ODD_SPLITS)
            wgt = _pad_cols(jnp.concatenate([wff, wmi, wmf], axis=1), GATE_LANES)
            gtb = _row(jnp.concatenate([fox_f_bias[j], mlstm_i_bias[j], mlstm_f_bias[j]]), GATE_LANES)
            wproj = (bf(wfq), bf(wfk), bf(wfv), bf(wgt), bf(wmq), bf(wmk), bf(wmv), bf(wmo), gtb,
                     _row(jnp.tile(fox_q_norm[j], n_heads)), _row(jnp.tile(fox_k_norm[j], n_heads)), bd)

            fq, fkt, fvt, gt, mq, mk, mv, so = _proj_odd(hp, gm, *wproj, tm=TM_PROJ, seq=sp)
            fq, gt, mq, mk, mv, so = [a.reshape(bp, sp, -1) for a in (fq, gt, mq, mk, mv, so)]
            oc_p = _fox_prompt_mixer(fq, fkt, fvt, gt)
            hm_p, c_p, n_p, m_p = _mlstm_mixer(mq, mk, mv, so, gt,
                                               jnp.zeros((bp, MLSTM_HEADS, MLSTM_HEAD_DIM, MLSTM_HEAD_DIM), F32),
                                               jnp.zeros((bp, MLSTM_HEADS, MLSTM_HEAD_DIM), F32),
                                               jnp.zeros((bp, MLSTM_HEADS), F32), mlstm_norm[j], t=PROMPT_CHUNK)
            odd_p.append((jnp.transpose(fkt.reshape(bp, n_heads, HEAD_DIM, sp), (0, 3, 1, 2)),
                          jnp.transpose(fvt.reshape(bp, n_heads, HEAD_DIM, sp), (0, 3, 1, 2)),
                          gt[..., :FOX_HEADS], c_p, n_p, m_p))

            fq, fk, fv, gt, mq, mk, mv, so = [a.reshape(bs, ss, -1) for a in _proj_odd(hs, gm, *wproj, tm=TM_PROJ)]
            oc_s = _fox_sample_mixer(fq, fk, fv, gt, cache_fox_k[j], cache_fox_v[j], cache_fox_logf[j])
            hm_s, c_s, n_s, m_s = _mlstm_mixer(mq, mk, mv, so, gt, state_mlstm_C[j], state_mlstm_n[j], state_mlstm_m[j],
                                               mlstm_norm[j], t=ss)
            odd_s.append((fk.reshape(bs, ss, n_heads, HEAD_DIM), fv.reshape(bs, ss, n_heads, HEAD_DIM),
                          gt[..., :FOX_HEADS], c_s, n_s, m_s))

            mix_p, mix_s = (oc_p, hm_p), (oc_s, hm_s)
            w_out = odd_w_out[j]

        ka = mix_p[0].shape[-1]
        wt = (bf(w_out[:ka]), bf(w_out[ka:]), _row(norm_ffn2[i]), bf(ffn2_wg[i]), bf(ffn2_wu[i]), bf(ffn2_wd[i]),
              bf(ple_proj[i]), _row(ple_norm[i]), _row(ple_gate_norm[i]), bf(ple_gate_w[i]))
        hp = _tail(hp, mix_p[0].reshape(bp * sp, -1), mix_p[1].reshape(bp * sp, -1),
                   p_prompt[i].reshape(bp * sp, PLE_DIM), *wt, tm=TM_PROJ)
        hs = _tail(hs, mix_s[0].reshape(bs * ss, -1), mix_s[1].reshape(bs * ss, -1),
                   p_sample[i].reshape(bs * ss, PLE_DIM), *wt, tm=TM_PROJ)

    def stk(states, idx):
        return jnp.stack([s[idx] for s in states])

    return (hp.reshape(bp, sp, D_MODEL), hs.reshape(bs, ss, D_MODEL),
            stk(even_p, 0), stk(even_s, 0), stk(even_p, 1), stk(even_s, 1),
            stk(even_p, 2), stk(even_s, 2), stk(even_p, 3), stk(even_s, 3),
            stk(odd_p, 0), stk(odd_s, 0), stk(odd_p, 1), stk(odd_s, 1), stk(odd_p, 2), stk(odd_s, 2),
            stk(odd_p, 3), stk(odd_s, 3), stk(odd_p, 4), stk(odd_s, 4), stk(odd_p, 5), stk(odd_s, 5))
```

```python
import functools

import numpy as np
import jax
import jax.numpy as jnp
from jax import lax
from jax.experimental import pallas as pl
from jax.experimental.pallas import tpu as pltpu

F32 = jnp.float32
BF16 = jnp.bfloat16
EPS = 1e-6
NEG_INF = -1e30

D_MODEL = 1024
D_FF = 2816
PLE_DIM = 256
HEAD_DIM = 64
CHUNK = 64
SSD_HEADS = 16
SSD_D_INNER = 1024
SSD_CONV_DIM = 1280
SSD_CONV = 4
BAND_HEADS = 8
BAND_PAST = 512
REL_CLIP = 128
FOX_HEADS = 8
MLSTM_HEADS = 4
MLSTM_HEAD_DIM = 128
ATT_WIDTH = 512
GATE_LANES = 128
EVEN_SPLITS = (1024, 1280, 16, 512, 512, 512)
ODD_SPLITS = (512, 512, 512, 8, 512, 512, 512, 4, 4, 512)

V7X_VMEM_BYTES = 64 * 1024 * 1024
VMEM_LIMIT = 56 * 1024 * 1024
FF_CHUNK = 256


def _cparams(n_axes):
    return pltpu.CompilerParams(dimension_semantics=("arbitrary",) * n_axes,
                                vmem_limit_bytes=VMEM_LIMIT)


def _resident(shape):
    nd = len(shape)
    return pl.BlockSpec(shape, lambda *_: (0,) * nd, pipeline_mode=pl.Buffered(1))


def _dot(a, b):
    return jnp.dot(a, b, preferred_element_type=F32)


def _dot_nt(a, b):
    return lax.dot_general(a, b, (((1,), (1,)), ((), ())), preferred_element_type=F32)


def _dot_tn(a, b):
    return lax.dot_general(a, b, (((0,), (0,)), ((), ())), preferred_element_type=F32)


def _split3(x):
    hi = x.astype(BF16)
    r = x - hi.astype(F32)
    mid = r.astype(BF16)
    lo = (r - mid.astype(F32)).astype(BF16)
    return hi, mid, lo


def _dot3_l(x, w01):
    hi, mid, lo = _split3(x)
    return _dot(hi, w01) + _dot(mid, w01) + _dot(lo, w01)


def _dot3_r(w01, x):
    hi, mid, lo = _split3(x)
    return _dot(w01, hi) + _dot(w01, mid) + _dot(w01, lo)


def _rms(x, g):
    ms = jnp.mean(x * x, axis=-1, keepdims=True)
    return x * lax.rsqrt(ms + EPS) * g


def _sigmoid(x):
    return 1.0 / (1.0 + jnp.exp(-x))


def _silu(x):
    return x * _sigmoid(x)


def _softplus(x):
    return jnp.maximum(x, 0.0) + jnp.log(1.0 + jnp.exp(-jnp.abs(x)))


def _log_sigmoid(x):
    return jnp.minimum(x, 0.0) - jnp.log(1.0 + jnp.exp(-jnp.abs(x)))


def _lower_tri(n):
    r = lax.broadcasted_iota(jnp.int32, (n, n), 0)
    c = lax.broadcasted_iota(jnp.int32, (n, n), 1)
    return r >= c


def _ffn_half(x, g_ref, wg_ref, wu_ref, wd_ref, acc_ref):
    xn = _rms(x, g_ref[...]).astype(BF16)
    for c in range(D_FF // FF_CHUNK):
        sl = slice(c * FF_CHUNK, (c + 1) * FF_CHUNK)
        gate = _dot(xn, wg_ref[:, sl])
        up = _dot(xn, wu_ref[:, sl])
        a = (_silu(gate) * up).astype(BF16)
        d = _dot(a, wd_ref[sl, :])
        if c == 0:
            acc_ref[...] = d
        else:
            acc_ref[...] += d
    return 0.5 * acc_ref[...]


def _ffn1_kernel(h_ref, g_ref, wg_ref, wu_ref, wd_ref, o_ref, acc_ref):
    x = h_ref[...]
    o_ref[...] = x + _ffn_half(x, g_ref, wg_ref, wu_ref, wd_ref, acc_ref)


def _ffn1(h, g, wg, wu, wd, tm):
    n = h.shape[0]
    tok = pl.BlockSpec((tm, D_MODEL), lambda i: (i, 0))
    return pl.pallas_call(
        _ffn1_kernel,
        grid=(n // tm,),
        in_specs=[tok, _resident((1, D_MODEL)), _resident((D_MODEL, D_FF)),
                  _resident((D_MODEL, D_FF)), _resident((D_FF, D_MODEL))],
        out_specs=tok,
        out_shape=jax.ShapeDtypeStruct((n, D_MODEL), F32),
        scratch_shapes=[pltpu.VMEM((tm, D_MODEL), F32)],
        compiler_params=_cparams(1),
        name="ffn1",
    )(h, g, wg, wu, wd)


def _head_rms(x, bd_ref, g, width):
    x2 = x * x
    hi = x2.astype(BF16)
    lo = (x2 - hi.astype(F32)).astype(BF16)
    ms = (_dot(hi, bd_ref[...]) + _dot(lo, bd_ref[...])) * (1.0 / width)
    return x * lax.rsqrt(ms + EPS) * g


def _proj_even_kernel(h_ref, gm_ref, wz_ref, wx_ref, wdt_ref, wq_ref, wk_ref, wv_ref,
                      dtb_ref, gq_ref, gk_ref, bd_ref,
                      z_ref, xbc_ref, dt_ref, q_ref, k_ref, v_ref):
    u = _rms(h_ref[...], gm_ref[...]).astype(BF16)
    z_ref[...] = _dot(u, wz_ref[...])
    xbc_ref[...] = _dot(u, wx_ref[...])
    dt_ref[...] = _softplus(_dot(u, wdt_ref[...]) + dtb_ref[...])
    q_ref[...] = _head_rms(_dot(u, wq_ref[...]), bd_ref, gq_ref[...], HEAD_DIM)
    k_ref[...] = _head_rms(_dot(u, wk_ref[...]), bd_ref, gk_ref[...], HEAD_DIM)
    v_ref[...] = _dot(u, wv_ref[...])


def _proj_even(h, gm, wz, wx, wdt, wq, wk, wv, dtb, gq, gk, bd, tm):
    n = h.shape[0]

    def tok(c):
        return pl.BlockSpec((tm, c), lambda i: (i, 0))

    widths = (D_MODEL, SSD_CONV_DIM, GATE_LANES, ATT_WIDTH, ATT_WIDTH, ATT_WIDTH)
    return pl.pallas_call(
        _proj_even_kernel,
        grid=(n // tm,),
        in_specs=[tok(D_MODEL), _resident((1, D_MODEL)),
                  _resident(wz.shape), _resident(wx.shape), _resident(wdt.shape),
                  _resident(wq.shape), _resident(wk.shape), _resident(wv.shape),
                  _resident(dtb.shape), _resident(gq.shape), _resident(gk.shape), _resident(bd.shape)],
        out_specs=[tok(c) for c in widths],
        out_shape=[jax.ShapeDtypeStruct((n, c), F32) for c in widths],
        compiler_params=_cparams(1),
        name="proj_even",
    )(h, gm, wz, wx, wdt, wq, wk, wv, dtb, gq, gk, bd)


def _proj_odd_kernel(h_ref, gm_ref, wfq_ref, wfk_ref, wfv_ref, wgt_ref, wmq_ref, wmk_ref, wmv_ref, wmo_ref,
                     gtb_ref, gq_ref, gk_ref, bd_ref,
                     fq_ref, fk_ref, fv_ref, gt_ref, mq_ref, mk_ref, mv_ref, so_ref, *, kv_transposed):
    u = _rms(h_ref[...], gm_ref[...]).astype(BF16)
    fq_ref[...] = _head_rms(_dot(u, wfq_ref[...]), bd_ref, gq_ref[...], HEAD_DIM)
    fk = _head_rms(_dot(u, wfk_ref[...]), bd_ref, gk_ref[...], HEAD_DIM)
    fv = _dot(u, wfv_ref[...])
    fk_ref[...] = fk.T if kv_transposed else fk
    fv_ref[...] = fv.T if kv_transposed else fv
    pre = _dot(u, wgt_ref[...]) + gtb_ref[...]
    lane = lax.broadcasted_iota(jnp.int32, pre.shape, 1)
    is_input_gate = (lane >= FOX_HEADS) & (lane < FOX_HEADS + MLSTM_HEADS)
    gt_ref[...] = jnp.where(is_input_gate, pre, _log_sigmoid(pre))
    mq_ref[...] = _dot(u, wmq_ref[...])
    mk_ref[...] = _dot(u, wmk_ref[...]) * (MLSTM_HEAD_DIM ** -0.5)
    mv_ref[...] = _dot(u, wmv_ref[...])
    so_ref[...] = _sigmoid(_dot(u, wmo_ref[...]))


def _proj_odd(h, gm, wfq, wfk, wfv, wgt, wmq, wmk, wmv, wmo, gtb, gq, gk, bd, tm, seq=None):
    n = h.shape[0]

    def tok(c):
        return pl.BlockSpec((tm, c), lambda i: (i, 0))

    widths = (ATT_WIDTH, ATT_WIDTH, ATT_WIDTH, GATE_LANES, ATT_WIDTH, ATT_WIDTH, ATT_WIDTH, ATT_WIDTH)
    out_specs = [tok(c) for c in widths]
    out_shape = [jax.ShapeDtypeStruct((n, c), F32) for c in widths]
    if seq is not None:
        per_seq = seq // tm
        for o in (1, 2):
            out_specs[o] = pl.BlockSpec((None, ATT_WIDTH, tm), lambda i: (i // per_seq, 0, i % per_seq))
            out_shape[o] = jax.ShapeDtypeStruct((n // seq, ATT_WIDTH, seq), F32)
    ws = (wfq, wfk, wfv, wgt, wmq, wmk, wmv, wmo, gtb, gq, gk, bd)
    return pl.pallas_call(
        functools.partial(_proj_odd_kernel, kv_transposed=seq is not None),
        grid=(n // tm,),
        in_specs=[tok(D_MODEL), _resident((1, D_MODEL))] + [_resident(w.shape) for w in ws],
        out_specs=out_specs,
        out_shape=out_shape,
        compiler_params=_cparams(1),
        name="proj_odd",
    )(h, gm, *ws)


def _tail_kernel(h_ref, a_ref, b_ref, p_ref, woa_ref, wob_ref, g2_ref, wg_ref, wu_ref, wd_ref,
                 wple_ref, gple_ref, ggate_ref, wgate_ref, o_ref, acc_ref):
    h = h_ref[...] + _dot(a_ref[...].astype(BF16), woa_ref[...]) + _dot(b_ref[...].astype(BF16), wob_ref[...])
    h = h + _ffn_half(h, g2_ref, wg_ref, wu_ref, wd_ref, acc_ref)
    e = _rms(_dot(p_ref[...].astype(BF16), wple_ref[...]), gple_ref[...])
    gate = _sigmoid(_dot(_rms(h, ggate_ref[...]).astype(BF16), wgate_ref[...]))
    o_ref[...] = h + e * gate


def _tail(h, a, b, p, woa, wob, g2, wg, wu, wd, wple, gple, ggate, wgate, tm):
    n = h.shape[0]

    def tok(c):
        return pl.BlockSpec((tm, c), lambda i: (i, 0))

    ws = (woa, wob, g2, wg, wu, wd, wple, gple, ggate, wgate)
    return pl.pallas_call(
        _tail_kernel,
        grid=(n // tm,),
        in_specs=[tok(D_MODEL), tok(a.shape[1]), tok(b.shape[1]), tok(PLE_DIM)] + [_resident(w.shape) for w in ws],
        out_specs=tok(D_MODEL),
        out_shape=jax.ShapeDtypeStruct((n, D_MODEL), F32),
        scratch_shapes=[pltpu.VMEM((tm, D_MODEL), F32)],
        compiler_params=_cparams(1),
        name="tail",
    )(h, a, b, p, *ws)


CONV_PAD = 8
SSD_PAIR = 128
SSD_GROUP_COLS = 512
SSD_STATE = 64


def _ssd_kernel(xbc_ref, z_ref, dt_ref, dtT_ref, conv0_ref, st0_ref, cw_ref, cb_ref,
                alog_ref, alogT_ref, dskip_ref, gn_ref, e_ref,
                y_ref, convo_ref, sto_ref, xp_ref, st_ref, *, tb, t):
    i = pl.program_id(1)

    @pl.when(i == 0)
    def _():
        xp_ref[0:CONV_PAD, :] = conv0_ref[...]
        st_ref[...] = jnp.zeros_like(st_ref)
        st_ref[0:SSD_STATE, 0:SSD_GROUP_COLS] = st0_ref[0]
        st_ref[SSD_STATE:2 * SSD_STATE, SSD_GROUP_COLS:2 * SSD_GROUP_COLS] = st0_ref[1]

    xp_ref[CONV_PAD:CONV_PAD + tb, :] = xbc_ref[...]
    cw = cw_ref[...]
    pre = cb_ref[...]
    for j in range(SSD_CONV):
        off = CONV_PAD - (SSD_CONV - 1) + j
        pre = pre + cw[j:j + 1, :] * xp_ref[off:off + tb, :]
    xc = _silu(pre)
    tail_rows = xp_ref[tb:tb + CONV_PAD, :]
    convo_ref[...] = tail_rows
    xp_ref[0:CONV_PAD, :] = tail_rows

    tri = _lower_tri(t)
    ltri = tri.astype(BF16)
    utri = (lax.broadcasted_iota(jnp.int32, (t, t), 0) <= lax.broadcasted_iota(jnp.int32, (t, t), 1)).astype(BF16)
    a_row = -jnp.exp(alog_ref[...])
    a_col = -jnp.exp(alogT_ref[...])
    e01 = e_ref[...]
    lane = lax.broadcasted_iota(jnp.int32, (t, SSD_PAIR), 1)
    st_r = lax.broadcasted_iota(jnp.int32, st_ref.shape, 0)
    st_c = lax.broadcasted_iota(jnp.int32, st_ref.shape, 1)
    st_own = (st_r < SSD_STATE) == (st_c < SSD_GROUP_COLS)

    for c in range(tb // t):
        r0 = c * t
        xs = xc[r0:r0 + t, 0:SSD_D_INNER]
        b2 = xc[r0:r0 + t, SSD_D_INNER:SSD_D_INNER + 128]
        c2 = xc[r0:r0 + t, SSD_D_INNER + 128:SSD_D_INNER + 256]
        dtc = dt_ref[r0:r0 + t, :]
        acum = _dot3_r(ltri, dtc * a_row)
        acum_t = _dot3_l(dtT_ref[:, r0:r0 + t] * a_col, utri)
        acum_x = _dot3_l(acum, e01)
        xdt = xs * _dot3_l(dtc, e01)
        last = acum_x[t - 1:t, :]
        tailx = jnp.exp(last - acum_x) * xdt
        b2b = b2.astype(BF16)
        y = _dot(c2.astype(BF16), st_ref[...].astype(BF16)) * jnp.exp(acum_x) + dskip_ref[...] * xs
        cbs = []
        for g in range(2):
            in_g = (lane >= HEAD_DIM * g) & (lane < HEAD_DIM * (g + 1))
            cbs.append(_dot_nt(jnp.where(in_g, c2, 0.0).astype(BF16), b2b))
        ys = []
        for j in range(SSD_HEADS // 2):
            xpair = xdt[:, SSD_PAIR * j:SSD_PAIR * (j + 1)]
            acc = None
            for s in range(2):
                hd = 2 * j + s
                seg = acum[:, hd:hd + 1] - acum_t[hd:hd + 1, :]
                m = jnp.exp(jnp.where(tri, seg, NEG_INF)) * cbs[j // 4]
                xh = jnp.where((lane < HEAD_DIM) if s == 0 else (lane >= HEAD_DIM), xpair, 0.0)
                d = _dot(m.astype(BF16), xh.astype(BF16))
                acc = d if acc is None else acc + d
            ys.append(acc)
        y = y + jnp.concatenate(ys, axis=1)
        upd = _dot_tn(b2b, tailx.astype(BF16))
        st_ref[...] = jnp.where(st_own, st_ref[...] * jnp.exp(last) + upd, 0.0)
        y_ref[r0:r0 + t, :] = _rms(y * _silu(z_ref[r0:r0 + t, :]), gn_ref[...])

    sto_ref[0] = st_ref[0:SSD_STATE, 0:SSD_GROUP_COLS]
    sto_ref[1] = st_ref[SSD_STATE:2 * SSD_STATE, SSD_GROUP_COLS:2 * SSD_GROUP_COLS]


def _ssd(xbc, z, dt, dt_t, conv0, st0, cw, cb, alog, alog_t, dskip, gn, e01, tb, t):
    bsz, n, _ = xbc.shape
    consts = (cw, cb, alog, alog_t, dskip, gn, e01)
    return pl.pallas_call(
        functools.partial(_ssd_kernel, tb=tb, t=t),
        grid=(bsz, n // tb),
        in_specs=[pl.BlockSpec((None, tb, SSD_CONV_DIM), lambda b, i: (b, i, 0)),
                  pl.BlockSpec((None, tb, SSD_D_INNER), lambda b, i: (b, i, 0)),
                  pl.BlockSpec((None, tb, GATE_LANES), lambda b, i: (b, i, 0)),
                  pl.BlockSpec((None, SSD_HEADS, tb), lambda b, i: (b, 0, i)),
                  pl.BlockSpec((None, CONV_PAD, SSD_CONV_DIM), lambda b, i: (b, 0, 0)),
                  pl.BlockSpec((None, 2, SSD_STATE, SSD_GROUP_COLS), lambda b, i: (b, 0, 0, 0))]
                 + [_resident(w.shape) for w in consts],
        out_specs=[pl.BlockSpec((None, tb, SSD_D_INNER), lambda b, i: (b, i, 0)),
                   pl.BlockSpec((None, CONV_PAD, SSD_CONV_DIM), lambda b, i: (b, 0, 0)),
                   pl.BlockSpec((None, 2, SSD_STATE, SSD_GROUP_COLS), lambda b, i: (b, 0, 0, 0))],
        out_shape=[jax.ShapeDtypeStruct((bsz, n, SSD_D_INNER), F32),
                   jax.ShapeDtypeStruct((bsz, CONV_PAD, SSD_CONV_DIM), F32),
                   jax.ShapeDtypeStruct((bsz, 2, SSD_STATE, SSD_GROUP_COLS), F32)],
        scratch_shapes=[pltpu.VMEM((CONV_PAD + tb, SSD_CONV_DIM), F32),
                        pltpu.VMEM((2 * SSD_STATE, 2 * SSD_GROUP_COLS), F32)],
        compiler_params=_cparams(2),
        name="ssd",
    )(xbc, z, dt, dt_t, conv0, st0, *consts)


def _head_expand(n_heads, width):
    e = np.zeros((GATE_LANES, n_heads * width), np.float32)
    for h in range(n_heads):
        e[h, h * width:(h + 1) * width] = 1.0
    return jnp.asarray(e, BF16)


def _ssd_mixer(xbc, z, dt, conv_buf, s0, conv_w, conv_b, a_log, d_skip, ssd_norm, tb, t):
    bsz = xbc.shape[0]
    dt_t = jnp.swapaxes(dt[..., :SSD_HEADS], 1, 2)
    conv0 = jnp.pad(conv_buf, ((0, 0), (CONV_PAD - (SSD_CONV - 1), 0), (0, 0)))
    st0 = s0.reshape(bsz, 2, 8, HEAD_DIM, SSD_STATE).transpose(0, 1, 4, 2, 3).reshape(bsz, 2, SSD_STATE, SSD_GROUP_COLS)
    cw = jnp.pad(conv_w, ((0, CONV_PAD - SSD_CONV), (0, 0)))
    alog = jnp.pad(a_log, (0, GATE_LANES - SSD_HEADS)).reshape(1, GATE_LANES)
    y, convo, sto = _ssd(xbc, z, dt, dt_t, conv0, st0, cw, conv_b.reshape(1, -1), alog, a_log.reshape(SSD_HEADS, 1),
                         jnp.repeat(d_skip, HEAD_DIM).reshape(1, -1), ssd_norm.reshape(1, -1),
                         _head_expand(SSD_HEADS, HEAD_DIM), tb, t)
    s_new = sto.reshape(bsz, 2, SSD_STATE, 8, HEAD_DIM).transpose(0, 1, 3, 4, 2).reshape(bsz, SSD_HEADS, HEAD_DIM, SSD_STATE)
    return y, convo[:, CONV_PAD - (SSD_CONV - 1):], s_new


GATE_I = FOX_HEADS
GATE_F = FOX_HEADS + MLSTM_HEADS


def _mlstm_kernel(q_ref, k_ref, v_ref, so_ref, gt_ref, gtT_ref, c0_ref, n0_ref, m0_ref, gn_ref,
                  h_ref, co_ref, no_ref, mo_ref, c_ref, n_ref, m_ref, *, t):
    i = pl.program_id(1)

    @pl.when(i == 0)
    def _():
        c_ref[...] = c0_ref[...]
        n_ref[...] = n0_ref[...]
        m_ref[...] = m0_ref[...]

    tri = _lower_tri(t)
    ltri = tri.astype(BF16)
    utri = (lax.broadcasted_iota(jnp.int32, (t, t), 0) <= lax.broadcasted_iota(jnp.int32, (t, t), 1)).astype(BF16)
    gt = gt_ref[...]
    gt_t = gtT_ref[...]
    cum = _dot3_r(ltri, gt)
    cum_t = _dot3_l(gt_t, utri)
    m_vec = m_ref[...]
    m_lane = lax.broadcasted_iota(jnp.int32, m_vec.shape, 1)
    m_next = m_vec
    for hh in range(MLSTM_HEADS):
        sl = slice(MLSTM_HEAD_DIM * hh, MLSTM_HEAD_DIM * (hh + 1))
        bcol = cum[:, GATE_F + hh:GATE_F + hh + 1]
        brow = cum_t[GATE_F + hh:GATE_F + hh + 1, :]
        ig_row = gt_t[GATE_I + hh:GATE_I + hh + 1, :]
        ig_col = gt[:, GATE_I + hh:GATE_I + hh + 1]
        dmat = jnp.where(tri, bcol - brow + ig_row, NEG_INF)
        g = bcol + m_vec[:, hh:hh + 1]
        m_t = jnp.maximum(g, jnp.max(dmat, axis=-1, keepdims=True))
        w = jnp.exp(dmat - m_t)
        inter = jnp.exp(g - m_t)
        qh = q_ref[:, sl]
        kh = k_ref[:, sl]
        vh = v_ref[:, sl]
        qb = qh.astype(BF16)
        kb = kh.astype(BF16)
        a = w * _dot_nt(qb, kb)
        cm = c_ref[hh]
        nv = n_ref[hh:hh + 1, :]
        num = inter * _dot(qb, cm.astype(BF16)) + _dot(a.astype(BF16), vh.astype(BF16))
        den = inter * jnp.sum(qh * nv, axis=-1, keepdims=True) + jnp.sum(a, axis=-1, keepdims=True)
        hout = num / jnp.maximum(jnp.abs(den), jnp.exp(-m_t))
        m_last = m_t[t - 1:t, :]
        inter_last = inter[t - 1:t, :]
        w_end = jnp.exp(bcol[t - 1:t, :] - bcol + ig_col - m_last)
        c_ref[hh] = inter_last * cm + _dot_tn(kb, (w_end * vh).astype(BF16))
        n_ref[hh:hh + 1, :] = inter_last * nv + jnp.sum(w_end * kh, axis=0, keepdims=True)
        m_next = jnp.where(m_lane == hh, m_last, m_next)
        h_ref[:, sl] = _rms(hout, gn_ref[:, sl]) * so_ref[:, sl]
    m_ref[...] = m_next
    co_ref[...] = c_ref[...]
    no_ref[...] = n_ref[...]
    mo_ref[...] = m_next


def _mlstm(q, k, v, so, gt, gt_t, c0, n0, m0, gn, t):
    bsz, n, _ = q.shape
    tok = pl.BlockSpec((None, t, ATT_WIDTH), lambda b, i: (b, i, 0))
    c_spec = pl.BlockSpec((None, MLSTM_HEADS, MLSTM_HEAD_DIM, MLSTM_HEAD_DIM), lambda b, i: (b, 0, 0, 0))
    n_spec = pl.BlockSpec((None, MLSTM_HEADS, MLSTM_HEAD_DIM), lambda b, i: (b, 0, 0))
    m_spec = pl.BlockSpec((None, 1, GATE_LANES), lambda b, i: (b, 0, 0))
    return pl.pallas_call(
        functools.partial(_mlstm_kernel, t=t),
        grid=(bsz, n // t),
        in_specs=[tok, tok, tok, tok,
                  pl.BlockSpec((None, t, GATE_LANES), lambda b, i: (b, i, 0)),
                  pl.BlockSpec((None, 16, t), lambda b, i: (b, 0, i)),
                  c_spec, n_spec, m_spec, _resident(gn.shape)],
        out_specs=[tok, c_spec, n_spec, m_spec],
        out_shape=[jax.ShapeDtypeStruct((bsz, n, ATT_WIDTH), F32),
                   jax.ShapeDtypeStruct(c0.shape, F32),
                   jax.ShapeDtypeStruct(n0.shape, F32),
                   jax.ShapeDtypeStruct(m0.shape, F32)],
        scratch_shapes=[pltpu.VMEM((MLSTM_HEADS, MLSTM_HEAD_DIM, MLSTM_HEAD_DIM), F32),
                        pltpu.VMEM((MLSTM_HEADS, MLSTM_HEAD_DIM), F32),
                        pltpu.VMEM((1, GATE_LANES), F32)],
        compiler_params=_cparams(2),
        name="mlstm",
    )(q, k, v, so, gt, gt_t, c0, n0, m0, gn)


def _mlstm_mixer(mq, mk, mv, so, gates, c0, n0, m0, ml_norm, t):
    gt_t = jnp.swapaxes(gates[..., :16], 1, 2)
    m0p = jnp.pad(m0, ((0, 0), (0, GATE_LANES - MLSTM_HEADS)))[:, None, :]
    h, c_new, n_new, m_new = _mlstm(mq, mk, mv, so, gates, gt_t, c0, n0, m0p, ml_norm.reshape(1, -1), t)
    return h, c_new, n_new, m_new[:, 0, :MLSTM_HEADS]


PAIR = 2 * HEAD_DIM
N_PAIRS = ATT_WIDTH // PAIR
QK_SCALE = HEAD_DIM ** -0.5


def _in_head(lane, h):
    return (lane >= HEAD_DIM * h) & (lane < HEAD_DIM * (h + 1))


def _stack_heads(q):
    lane = lax.broadcasted_iota(jnp.int32, q.shape, 1)
    return jnp.concatenate([jnp.where(_in_head(lane, h), q, 0.0) for h in range(ATT_WIDTH // HEAD_DIM)], axis=0)


def _unstack_heads(o, nq):
    lane = lax.broadcasted_iota(jnp.int32, (nq, ATT_WIDTH), 1)
    out = o[0:nq, :]
    for h in range(1, ATT_WIDTH // HEAD_DIM):
        out = jnp.where(_in_head(lane, h), o[h * nq:(h + 1) * nq, :], out)
    return out


def _rows_per_head(x_t, nq):
    n = x_t.shape[1]
    return jnp.concatenate([jnp.broadcast_to(x_t[h:h + 1, :], (nq, n)) for h in range(x_t.shape[0])], axis=0)


BAND_QB = BAND_PAST // 2
BAND_STEP_HEADS = 4


def _band_prompt_kernel(q_ref, k2_ref, k1_ref, k0_ref, v2_ref, v1_ref, v0_ref, bias_ref, o_ref):
    i = pl.program_id(2)
    qb, width = q_ref.shape
    lane = lax.broadcasted_iota(jnp.int32, (qb, width), 1)
    q = q_ref[...] * QK_SCALE
    kcat = jnp.concatenate([k2_ref[...], k1_ref[...], k0_ref[...]], axis=0).astype(BF16)
    vcat = jnp.concatenate([v2_ref[...], v1_ref[...], v0_ref[...]], axis=0).astype(BF16)
    col = lax.broadcasted_iota(jnp.int32, (qb, 3 * qb), 1)
    in_sequence = col >= (2 - i) * qb
    out = None
    for s in range(width // HEAD_DIM):
        qh = jnp.where(_in_head(lane, s), q, 0.0).astype(BF16)
        sc = jnp.where(in_sequence, _dot_nt(qh, kcat) + bias_ref[s], NEG_INF)
        p = jnp.exp(sc - jnp.max(sc, axis=-1, keepdims=True))
        o = _dot(p.astype(BF16), vcat) / jnp.sum(p, axis=-1, keepdims=True)
        out = o if out is None else jnp.where(_in_head(lane, s), o, out)
    o_ref[...] = out


def _band_prompt(q, k, v, bias):
    bsz, n, _ = q.shape
    qb = BAND_QB

    width = BAND_STEP_HEADS * HEAD_DIM

    def kv_spec(back):
        return pl.BlockSpec((None, qb, width), lambda b, j, i: (b, jnp.maximum(i - back, 0), j))

    return pl.pallas_call(
        _band_prompt_kernel,
        grid=(bsz, ATT_WIDTH // width, n // qb),
        in_specs=[kv_spec(0), kv_spec(2), kv_spec(1), kv_spec(0), kv_spec(2), kv_spec(1), kv_spec(0),
                  pl.BlockSpec((BAND_STEP_HEADS, qb, 3 * qb), lambda b, j, i: (j, 0, 0))],
        out_specs=kv_spec(0),
        out_shape=jax.ShapeDtypeStruct((bsz, n, ATT_WIDTH), F32),
        compiler_params=_cparams(3),
        name="band_prompt",
    )(q, k, k, k, v, v, v, bias)


def _rel_bias_block(rel_table, rows, cols, offset):
    period = rows + cols - 1
    dist = offset + rows - 1 - np.arange(period)
    profile = rel_table[:, np.clip(dist, -REL_CLIP, REL_CLIP) + REL_CLIP].astype(F32)
    rolled = jnp.roll(profile, -(rows - 1), axis=1)
    flat = jnp.tile(rolled, (1, rows))[:, :rows * (period - 1)]
    return flat.reshape(-1, rows, period - 1)[:, :, :cols]


def _band_prompt_bias(rel_table):
    qb = BAND_QB
    r = np.arange(qb)[:, None] + 2 * qb
    c = np.arange(3 * qb)[None, :]
    in_band = (c // CHUNK <= r // CHUNK) & (c // CHUNK >= r // CHUNK - BAND_PAST // CHUNK)
    return jnp.where(jnp.asarray(in_band)[None], _rel_bias_block(rel_table, qb, 3 * qb, 2 * qb), NEG_INF)


def _band_sample_kernel(q_ref, kct_ref, vct_ref, kn_ref, vn_ref, bc_ref, bn_ref, o_ref):
    nq = q_ref.shape[0]
    qs = _stack_heads(q_ref[...] * QK_SCALE).astype(BF16)
    s_c = _dot(qs, kct_ref[...].astype(BF16)) + bc_ref[...]
    s_n = _dot_nt(qs, kn_ref[...].astype(BF16)) + bn_ref[...]
    m = jnp.maximum(jnp.max(s_c, axis=-1, keepdims=True), jnp.max(s_n, axis=-1, keepdims=True))
    p_c = jnp.exp(s_c - m)
    p_n = jnp.exp(s_n - m)
    l = jnp.sum(p_c, axis=-1, keepdims=True) + jnp.sum(p_n, axis=-1, keepdims=True)
    o = _dot_nt(p_c.astype(BF16), vct_ref[...].astype(BF16)) + _dot(p_n.astype(BF16), vn_ref[...].astype(BF16))
    o_ref[...] = _unstack_heads(o / l, nq)


def _band_sample(q, k, v, kt_cache, vt_cache, bias_c, bias_n):
    bsz, nq, _ = q.shape
    nb = kt_cache.shape[2]
    new = pl.BlockSpec((None, nq, ATT_WIDTH), lambda b: (b, 0, 0))
    old = pl.BlockSpec((None, ATT_WIDTH, nb), lambda b: (b, 0, 0))
    return pl.pallas_call(
        _band_sample_kernel,
        grid=(bsz,),
        in_specs=[new, old, old, new, new, _resident(bias_c.shape), _resident(bias_n.shape)],
        out_specs=new,
        out_shape=jax.ShapeDtypeStruct((bsz, nq, ATT_WIDTH), F32),
        compiler_params=_cparams(1),
        name="band_sample",
    )(q, kt_cache, vt_cache, k, v, bias_c, bias_n)


def _band_sample_bias(rel_table, nq, nb):
    bias = _rel_bias_block(rel_table, nq, nb + nq, nb).reshape(BAND_HEADS * nq, nb + nq)
    return bias[:, :nb], bias[:, nb:]


def _cumsum_lanes_kernel(x_ref, o_ref, carry_ref):
    lb = x_ref.shape[1]

    @pl.when(pl.program_id(0) == 0)
    def _():
        carry_ref[...] = jnp.zeros_like(carry_ref)

    utri = (lax.broadcasted_iota(jnp.int32, (lb, lb), 0) <= lax.broadcasted_iota(jnp.int32, (lb, lb), 1)).astype(BF16)
    c = _dot3_l(x_ref[...], utri) + carry_ref[...]
    o_ref[...] = c
    carry_ref[...] = c[:, lb - 1:lb]


def _cumsum_lanes(x, lb):
    rows, n = x.shape
    blk = pl.BlockSpec((rows, lb), lambda i: (0, i))
    return pl.pallas_call(
        _cumsum_lanes_kernel,
        grid=(n // lb,),
        in_specs=[blk],
        out_specs=blk,
        out_shape=jax.ShapeDtypeStruct((rows, n), F32),
        scratch_shapes=[pltpu.VMEM((rows, 1), F32)],
        compiler_params=_cparams(1),
        name="cumsum_lanes",
    )(x)


FOX_BLK = 512
FOX_UNROLL = 4


LOG2E = 1.4426950408889634


def _fox_prompt_kernel(q_ref, kt_ref, vt_ref, cq_ref, ck_ref, o_ref, m_ref, acc_ref):
    i = pl.program_id(2)
    blk = q_ref.shape[0]
    lane = lax.broadcasted_iota(jnp.int32, (blk, PAIR), 1)
    q = q_ref[...] * (QK_SCALE * LOG2E)
    q2 = jnp.concatenate([jnp.where(_in_head(lane, s), q, 0.0) for s in range(2)], axis=0).astype(BF16)
    cq = cq_ref[...] * LOG2E
    cqb = jnp.concatenate([jnp.broadcast_to(cq[:, s:s + 1], (blk, PAIR)) for s in range(2)], axis=0)
    tri = _lower_tri(blk)
    tri2 = jnp.concatenate([tri, tri], axis=0)
    ones = jnp.ones((PAIR, blk), BF16)
    m_ref[...] = jnp.full_like(m_ref, NEG_INF)
    acc_ref[...] = jnp.zeros_like(acc_ref)

    def step(kk, on_diagonal):
        k0 = pl.multiple_of(kk * blk, blk)
        kb = kt_ref[:, pl.ds(k0, blk)].astype(BF16)
        vb = jnp.concatenate([vt_ref[:, pl.ds(k0, blk)].astype(BF16), ones], axis=0)
        ck = ck_ref[:, pl.ds(k0, blk)] * LOG2E
        sc = _dot(q2, kb)
        sc = jnp.concatenate([sc[:blk] - ck[0:1], sc[blk:] - ck[1:2]], axis=0)
        if on_diagonal:
            sc = jnp.where(tri2, sc, NEG_INF)
        chunks = [sc[:, PAIR * c:PAIR * (c + 1)] for c in range(blk // PAIR)]
        rm = chunks[0]
        for ch in chunks[1:]:
            rm = jnp.maximum(rm, ch)
        rmb = jnp.broadcast_to(jnp.max(rm, axis=-1, keepdims=True), (2 * blk, PAIR))
        m_old = m_ref[...]
        m_new = jnp.maximum(m_old, rmb + cqb)
        shift = m_new - cqb
        p = jnp.concatenate([jnp.exp2(ch - shift) for ch in chunks], axis=1).astype(BF16)
        alpha = jnp.exp2(m_old - m_new)
        acc_ref[...] = jnp.concatenate([alpha, alpha], axis=1) * acc_ref[...] + _dot_nt(p, vb)
        m_ref[...] = m_new

    def body(quad, carry):
        for u in range(FOX_UNROLL):
            step(FOX_UNROLL * quad + u, False)
        return carry

    lax.fori_loop(0, lax.shift_right_logical(i, FOX_UNROLL.bit_length() - 1), body, 0)
    done = jnp.bitwise_and(i, -FOX_UNROLL)
    width = FOX_UNROLL // 2
    while width >= 1:
        @pl.when(jnp.bitwise_and(i, width) != 0)
        def _(done=done, width=width):
            for u in range(width):
                step(done + u, False)
        done = done + jnp.bitwise_and(i, width)
        width //= 2

    step(i, True)
    acc = acc_ref[...]
    o = acc[:, :PAIR] / acc[:, PAIR:]
    o_ref[...] = jnp.where(lane < HEAD_DIM, o[:blk], o[blk:])


def _fox_prompt(q, kt, vt, ct_col, ct_row):
    bsz, n, _ = q.shape
    blk = FOX_BLK
    tile = pl.BlockSpec((None, blk, PAIR), lambda b, j, i: (b, i, j))
    whole = pl.BlockSpec((None, PAIR, n), lambda b, j, i: (b, j, 0))
    return pl.pallas_call(
        _fox_prompt_kernel,
        grid=(bsz, N_PAIRS, n // blk),
        in_specs=[tile, whole, whole,
                  pl.BlockSpec((None, None, blk, 2), lambda b, j, i: (b, j, i, 0)),
                  pl.BlockSpec((None, None, 2, n), lambda b, j, i: (b, j, 0, 0))],
        out_specs=tile,
        out_shape=jax.ShapeDtypeStruct((bsz, n, ATT_WIDTH), F32),
        scratch_shapes=[pltpu.VMEM((2 * blk, PAIR), F32), pltpu.VMEM((2 * blk, 2 * PAIR), F32)],
        compiler_params=_cparams(3),
        name="fox_prompt",
    )(q, kt, vt, ct_col, ct_row)


FOX_KV_BLK = 2048


def _fox_sample_kernel(q_ref, kct_ref, vct_ref, kn_ref, vn_ref, cq_ref, ckc_ref, ckn_ref, o_ref, m_ref, l_ref, acc_ref):
    j = pl.program_id(1)
    nq = q_ref.shape[0]

    @pl.when(j == 0)
    def _():
        m_ref[...] = jnp.full_like(m_ref, NEG_INF)
        l_ref[...] = jnp.zeros_like(l_ref)
        acc_ref[...] = jnp.zeros_like(acc_ref)

    qs = _stack_heads(q_ref[...] * QK_SCALE).astype(BF16)
    cq = cq_ref[...]

    def absorb(sc, pv):
        m_old = m_ref[...]
        m_new = jnp.maximum(m_old, jnp.max(sc, axis=-1, keepdims=True))
        alpha = jnp.exp(m_old - m_new)
        p = jnp.exp(sc - m_new)
        l_ref[...] = alpha * l_ref[...] + jnp.sum(p, axis=-1, keepdims=True)
        acc_ref[...] = alpha * acc_ref[...] + pv(p.astype(BF16))
        m_ref[...] = m_new

    vct = vct_ref[...].astype(BF16)
    absorb(_dot(qs, kct_ref[...].astype(BF16)) + (cq - _rows_per_head(ckc_ref[...], nq)), lambda p: _dot_nt(p, vct))

    @pl.when(j == pl.num_programs(1) - 1)
    def _():
        sc = _dot_nt(qs, kn_ref[...].astype(BF16)) + (cq - _rows_per_head(ckn_ref[...], nq))
        row = lax.broadcasted_iota(jnp.int32, sc.shape, 0)
        col = lax.broadcasted_iota(jnp.int32, sc.shape, 1)
        vn = vn_ref[...].astype(BF16)
        absorb(jnp.where(col <= row % nq, sc, NEG_INF), lambda p: _dot(p, vn))
        o_ref[...] = _unstack_heads(acc_ref[...] / l_ref[...], nq)


def _fox_sample(q, k, v, kt_cache, vt_cache, cq, ck_cache, ck_new):
    bsz, nq, _ = q.shape
    npast = kt_cache.shape[2]
    kvb = FOX_KV_BLK
    new = pl.BlockSpec((None, nq, ATT_WIDTH), lambda b, j: (b, 0, 0))
    old = pl.BlockSpec((None, ATT_WIDTH, kvb), lambda b, j: (b, 0, j))
    rows = FOX_HEADS * nq
    return pl.pallas_call(
        _fox_sample_kernel,
        grid=(bsz, npast // kvb),
        in_specs=[new, old, old, new, new,
                  pl.BlockSpec((None, rows, 1), lambda b, j: (b, 0, 0)),
                  pl.BlockSpec((None, FOX_HEADS, kvb), lambda b, j: (b, 0, j)),
                  pl.BlockSpec((None, FOX_HEADS, nq), lambda b, j: (b, 0, 0))],
        out_specs=new,
        out_shape=jax.ShapeDtypeStruct((bsz, nq, ATT_WIDTH), F32),
        scratch_shapes=[pltpu.VMEM((rows, 1), F32), pltpu.VMEM((rows, 1), F32), pltpu.VMEM((rows, ATT_WIDTH), F32)],
        compiler_params=_cparams(2),
        name="fox_sample",
    )(q, kt_cache, vt_cache, k, v, cq, ck_cache, ck_new)


LANES = 128


def _fox_prompt_mixer(fq, fk, fv, gates):
    bsz, n, _ = fq.shape
    logf_t = jnp.swapaxes(gates[..., :FOX_HEADS], 1, 2).reshape(bsz * FOX_HEADS, n)
    ct = _cumsum_lanes(logf_t, FOX_BLK).reshape(bsz, N_PAIRS, 2, n)
    return _fox_prompt(fq, fk, fv, jnp.swapaxes(ct, 2, 3), ct)


def _cache_t(cache):
    bsz, npast = cache.shape[:2]
    return jnp.transpose(cache, (0, 2, 3, 1)).reshape(bsz, ATT_WIDTH, npast)


def _fox_sample_mixer(fq, fk, fv, gates, k_cache, v_cache, lf_cache):
    bsz, nq, _ = fq.shape
    npast = k_cache.shape[1]
    lf_all = jnp.concatenate([jnp.swapaxes(lf_cache, 1, 2), jnp.swapaxes(gates[..., :FOX_HEADS], 1, 2)], axis=2)
    total = npast + nq
    padded = -(-total // LANES) * LANES
    lb = next(c for c in (512, 384, 256, 128) if padded % c == 0)
    lf_all = jnp.pad(lf_all, ((0, 0), (0, 0), (0, padded - total))).reshape(bsz * FOX_HEADS, padded)
    ct = _cumsum_lanes(lf_all, lb).reshape(bsz, FOX_HEADS, padded)
    ck_new = ct[:, :, npast:total]
    return _fox_sample(fq, fk, fv, _cache_t(k_cache), _cache_t(v_cache), ck_new.reshape(bsz, FOX_HEADS * nq, 1),
                       ct[:, :, :npast], ck_new)


TM_FFN = 1024
TM_PROJ = 512
SSD_PROMPT_CHUNK = 128
MLSTM_PROMPT_CHUNK = 256
SSD_BLOCK_ROWS = 512


def _block_diag_ones(n, width):
    idx = np.arange(n) // width
    return jnp.asarray(idx[:, None] == idx[None, :], BF16)


def _cols(w, splits):
    cuts = np.concatenate([[0], np.cumsum(splits)])
    return [w[:, int(cuts[i]):int(cuts[i + 1])] for i in range(len(splits))]


def _pad_cols(w, width):
    return jnp.pad(w, ((0, 0), (0, width - w.shape[1])))


def _row(v, width=None):
    v = v.reshape(1, -1).astype(F32)
    return v if width is None else _pad_cols(v, width)


def kernel(x_prompt, x_sample, p_prompt, p_sample, state_ssd_conv, state_ssd, cache_band_k, cache_band_v, cache_fox_k, cache_fox_v, cache_fox_logf, state_mlstm_C, state_mlstm_n, state_mlstm_m, norm_ffn1, ffn1_wg, ffn1_wu, ffn1_wd, norm_mix, norm_ffn2, ffn2_wg, ffn2_wu, ffn2_wd, ple_proj, ple_norm, ple_gate_norm, ple_gate_w, even_w_in, even_w_out, ssd_conv_w, ssd_conv_b, ssd_a_log, ssd_dt_bias, ssd_d, ssd_norm, band_q_norm, band_k_norm, band_rel_bias, odd_w_in, odd_w_out, fox_q_norm, fox_k_norm, fox_f_bias, mlstm_i_bias, mlstm_f_bias, mlstm_norm):
    bp, sp, _ = x_prompt.shape
    bs, ss, _ = x_sample.shape
    depth = norm_ffn1.shape[0]
    n_heads = ATT_WIDTH // HEAD_DIM
    assert ss <= CHUNK and sp % max(SSD_BLOCK_ROWS, FOX_BLK, TM_FFN) == 0 and (bs * ss) % TM_PROJ == 0

    def bf(w):
        return w.astype(BF16)

    bd = _block_diag_ones(ATT_WIDTH, HEAD_DIM)
    hp = x_prompt.reshape(bp * sp, D_MODEL)
    hs = x_sample.reshape(bs * ss, D_MODEL)
    tm_s = min(TM_FFN, bs * ss)
    even_p, even_s, odd_p, odd_s = [], [], [], []

    for i in range(depth):
        j = i // 2
        w1 = (_row(norm_ffn1[i]), bf(ffn1_wg[i]), bf(ffn1_wu[i]), bf(ffn1_wd[i]))
        hp = _ffn1(hp, *w1, tm=TM_FFN)
        hs = _ffn1(hs, *w1, tm=tm_s)
        gm = _row(norm_mix[i])

        if i % 2 == 0:
            wz, wx, wdt, wq, wk, wv = _cols(even_w_in[j], EVEN_SPLITS)
            wproj = (bf(wz), bf(wx), bf(_pad_cols(wdt, GATE_LANES)), bf(wq), bf(wk), bf(wv),
                     _row(ssd_dt_bias[j], GATE_LANES), _row(jnp.tile(band_q_norm[j], n_heads)),
                     _row(jnp.tile(band_k_norm[j], n_heads)), bd)
            ssd_w = (ssd_conv_w[j], ssd_conv_b[j], ssd_a_log[j], ssd_d[j], ssd_norm[j])

            z, xbc, dt, q, k, v = [a.reshape(bp, sp, -1) for a in _proj_even(hp, gm, *wproj, tm=TM_PROJ)]
            ya_p, conv_p, st_p = _ssd_mixer(xbc, z, dt, jnp.zeros((bp, SSD_CONV - 1, SSD_CONV_DIM), F32),
                                            jnp.zeros((bp, SSD_HEADS, HEAD_DIM, SSD_STATE), F32), *ssd_w,
                                            tb=SSD_BLOCK_ROWS, t=SSD_PROMPT_CHUNK)
            ob_p = _band_prompt(q, k, v, _band_prompt_bias(band_rel_bias[j]))
            keep = min(BAND_PAST, sp)
            even_p.append((conv_p, st_p, k[:, sp - keep:].reshape(bp, keep, n_heads, HEAD_DIM),
                           v[:, sp - keep:].reshape(bp, keep, n_heads, HEAD_DIM)))

            z, xbc, dt, q, k, v = [a.reshape(bs, ss, -1) for a in _proj_even(hs, gm, *wproj, tm=TM_PROJ)]
            ya_s, conv_s, st_s = _ssd_mixer(xbc, z, dt, state_ssd_conv[j], state_ssd[j], *ssd_w, tb=ss, t=ss)
            nb = cache_band_k.shape[2]
            ob_s = _band_sample(q, k, v, _cache_t(cache_band_k[j]), _cache_t(cache_band_v[j]),
                                *_band_sample_bias(band_rel_bias[j], ss, nb))
            even_s.append((conv_s, st_s, k.reshape(bs, ss, n_heads, HEAD_DIM), v.reshape(bs, ss, n_heads, HEAD_DIM)))

            mix_p, mix_s = (ya_p, ob_p), (ya_s, ob_s)
            w_out = even_w_out[j]
        else:
            wfq, wfk, wfv, wff, wmq, wmk, wmv, wmi, wmf, wmo = _cols(odd_w_in[j], ODD_SPLITS)
            wgt = _pad_cols(jnp.concatenate([wff, wmi, wmf], axis=1), GATE_LANES)
            gtb = _row(jnp.concatenate([fox_f_bias[j], mlstm_i_bias[j], mlstm_f_bias[j]]), GATE_LANES)
            wproj = (bf(wfq), bf(wfk), bf(wfv), bf(wgt), bf(wmq), bf(wmk), bf(wmv), bf(wmo), gtb,
                     _row(jnp.tile(fox_q_norm[j], n_heads)), _row(jnp.tile(fox_k_norm[j], n_heads)), bd)

            fq, fkt, fvt, gt, mq, mk, mv, so = _proj_odd(hp, gm, *wproj, tm=TM_PROJ, seq=sp)
            fq, gt, mq, mk, mv, so = [a.reshape(bp, sp, -1) for a in (fq, gt, mq, mk, mv, so)]
            oc_p = _fox_prompt_mixer(fq, fkt, fvt, gt)
            hm_p, c_p, n_p, m_p = _mlstm_mixer(mq, mk, mv, so, gt,
                                               jnp.zeros((bp, MLSTM_HEADS, MLSTM_HEAD_DIM, MLSTM_HEAD_DIM), F32),
                                               jnp.zeros((bp, MLSTM_HEADS, MLSTM_HEAD_DIM), F32),
                                               jnp.zeros((bp, MLSTM_HEADS), F32), mlstm_norm[j], t=MLSTM_PROMPT_CHUNK)
            odd_p.append((jnp.transpose(fkt.reshape(bp, n_heads, HEAD_DIM, sp), (0, 3, 1, 2)),
                          jnp.transpose(fvt.reshape(bp, n_heads, HEAD_DIM, sp), (0, 3, 1, 2)),
                          gt[..., :FOX_HEADS], c_p, n_p, m_p))

            fq, fk, fv, gt, mq, mk, mv, so = [a.reshape(bs, ss, -1) for a in _proj_odd(hs, gm, *wproj, tm=TM_PROJ)]
            oc_s = _fox_sample_mixer(fq, fk, fv, gt, cache_fox_k[j], cache_fox_v[j], cache_fox_logf[j])
            hm_s, c_s, n_s, m_s = _mlstm_mixer(mq, mk, mv, so, gt, state_mlstm_C[j], state_mlstm_n[j], state_mlstm_m[j],
                                               mlstm_norm[j], t=ss)
            odd_s.append((fk.reshape(bs, ss, n_heads, HEAD_DIM), fv.reshape(bs, ss, n_heads, HEAD_DIM),
                          gt[..., :FOX_HEADS], c_s, n_s, m_s))

            mix_p, mix_s = (oc_p, hm_p), (oc_s, hm_s)
            w_out = odd_w_out[j]

        ka = mix_p[0].shape[-1]
        wt = (bf(w_out[:ka]), bf(w_out[ka:]), _row(norm_ffn2[i]), bf(ffn2_wg[i]), bf(ffn2_wu[i]), bf(ffn2_wd[i]),
              bf(ple_proj[i]), _row(ple_norm[i]), _row(ple_gate_norm[i]), bf(ple_gate_w[i]))
        hp = _tail(hp, mix_p[0].reshape(bp * sp, -1), mix_p[1].reshape(bp * sp, -1),
                   p_prompt[i].reshape(bp * sp, PLE_DIM), *wt, tm=TM_PROJ)
        hs = _tail(hs, mix_s[0].reshape(bs * ss, -1), mix_s[1].reshape(bs * ss, -1),
                   p_sample[i].reshape(bs * ss, PLE_DIM), *wt, tm=TM_PROJ)

    def stk(states, idx):
        return jnp.stack([s[idx] for s in states])

    return (hp.reshape(bp, sp, D_MODEL), hs.reshape(bs, ss, D_MODEL),
            stk(even_p, 0), stk(even_s, 0), stk(even_p, 1), stk(even_s, 1),
            stk(even_p, 2), stk(even_s, 2), stk(even_p, 3), stk(even_s, 3),
            stk(odd_p, 0), stk(odd_s, 0), stk(odd_p, 1), stk(odd_s, 1), stk(odd_p, 2), stk(odd_s, 2),
            stk(odd_p, 3), stk(odd_s, 3), stk(odd_p, 4), stk(odd_s, 4), stk(odd_p, 5), stk(odd_s, 5))
```

```python
import functools

import numpy as np
import jax
import jax.numpy as jnp
from jax import lax
from jax.experimental import pallas as pl
from jax.experimental.pallas import tpu as pltpu

F32 = jnp.float32
BF16 = jnp.bfloat16
EPS = 1e-6
NEG_INF = -1e30

D_MODEL = 1024
D_FF = 2816
PLE_DIM = 256
HEAD_DIM = 64
CHUNK = 64
SSD_HEADS = 16
SSD_D_INNER = 1024
SSD_CONV_DIM = 1280
SSD_CONV = 4
BAND_HEADS = 8
BAND_PAST = 512
REL_CLIP = 128
FOX_HEADS = 8
MLSTM_HEADS = 4
MLSTM_HEAD_DIM = 128
ATT_WIDTH = 512
GATE_LANES = 128
EVEN_SPLITS = (1024, 1280, 16, 512, 512, 512)
ODD_SPLITS = (512, 512, 512, 8, 512, 512, 512, 4, 4, 512)

V7X_VMEM_BYTES = 64 * 1024 * 1024
VMEM_LIMIT = 56 * 1024 * 1024
FF_CHUNK = 256


def _cparams(n_axes):
    return pltpu.CompilerParams(dimension_semantics=("arbitrary",) * n_axes,
                                vmem_limit_bytes=VMEM_LIMIT)


def _resident(shape):
    nd = len(shape)
    return pl.BlockSpec(shape, lambda *_: (0,) * nd, pipeline_mode=pl.Buffered(1))


def _dot(a, b):
    return jnp.dot(a, b, preferred_element_type=F32)


def _dot_nt(a, b):
    return lax.dot_general(a, b, (((1,), (1,)), ((), ())), preferred_element_type=F32)


def _dot_tn(a, b):
    return lax.dot_general(a, b, (((0,), (0,)), ((), ())), preferred_element_type=F32)


def _split3(x):
    hi = x.astype(BF16)
    r = x - hi.astype(F32)
    mid = r.astype(BF16)
    lo = (r - mid.astype(F32)).astype(BF16)
    return hi, mid, lo


def _dot3_l(x, w01):
    hi, mid, lo = _split3(x)
    return _dot(hi, w01) + _dot(mid, w01) + _dot(lo, w01)


def _dot3_r(w01, x):
    hi, mid, lo = _split3(x)
    return _dot(w01, hi) + _dot(w01, mid) + _dot(w01, lo)


def _rms(x, g):
    ms = jnp.mean(x * x, axis=-1, keepdims=True)
    return x * lax.rsqrt(ms + EPS) * g


def _sigmoid(x):
    return 1.0 / (1.0 + jnp.exp(-x))


def _silu(x):
    return x * _sigmoid(x)


def _softplus(x):
    return jnp.maximum(x, 0.0) + jnp.log(1.0 + jnp.exp(-jnp.abs(x)))


def _log_sigmoid(x):
    return jnp.minimum(x, 0.0) - jnp.log(1.0 + jnp.exp(-jnp.abs(x)))


def _lower_tri(n):
    r = lax.broadcasted_iota(jnp.int32, (n, n), 0)
    c = lax.broadcasted_iota(jnp.int32, (n, n), 1)
    return r >= c


def _ffn_half(x, g_ref, wg_ref, wu_ref, wd_ref, acc_ref):
    xn = _rms(x, g_ref[...]).astype(BF16)
    for c in range(D_FF // FF_CHUNK):
        sl = slice(c * FF_CHUNK, (c + 1) * FF_CHUNK)
        gate = _dot(xn, wg_ref[:, sl])
        up = _dot(xn, wu_ref[:, sl])
        a = (_silu(gate) * up).astype(BF16)
        d = _dot(a, wd_ref[sl, :])
        if c == 0:
            acc_ref[...] = d
        else:
            acc_ref[...] += d
    return 0.5 * acc_ref[...]


def _ffn1_kernel(h_ref, g_ref, wg_ref, wu_ref, wd_ref, o_ref, acc_ref):
    x = h_ref[...]
    o_ref[...] = x + _ffn_half(x, g_ref, wg_ref, wu_ref, wd_ref, acc_ref)


def _ffn1(h, g, wg, wu, wd, tm):
    n = h.shape[0]
    tok = pl.BlockSpec((tm, D_MODEL), lambda i: (i, 0))
    return pl.pallas_call(
        _ffn1_kernel,
        grid=(n // tm,),
        in_specs=[tok, _resident((1, D_MODEL)), _resident((D_MODEL, D_FF)),
                  _resident((D_MODEL, D_FF)), _resident((D_FF, D_MODEL))],
        out_specs=tok,
        out_shape=jax.ShapeDtypeStruct((n, D_MODEL), F32),
        scratch_shapes=[pltpu.VMEM((tm, D_MODEL), F32)],
        compiler_params=_cparams(1),
        name="ffn1",
    )(h, g, wg, wu, wd)


def _head_rms(x, bd_ref, g, width):
    x2 = x * x
    hi = x2.astype(BF16)
    lo = (x2 - hi.astype(F32)).astype(BF16)
    ms = (_dot(hi, bd_ref[...]) + _dot(lo, bd_ref[...])) * (1.0 / width)
    return x * lax.rsqrt(ms + EPS) * g


def _proj_even_kernel(h_ref, gm_ref, wz_ref, wx_ref, wdt_ref, wq_ref, wk_ref, wv_ref,
                      dtb_ref, gq_ref, gk_ref, bd_ref,
                      z_ref, xbc_ref, dt_ref, q_ref, k_ref, v_ref):
    u = _rms(h_ref[...], gm_ref[...]).astype(BF16)
    z_ref[...] = _dot(u, wz_ref[...])
    xbc_ref[...] = _dot(u, wx_ref[...])
    dt_ref[...] = _softplus(_dot(u, wdt_ref[...]) + dtb_ref[...])
    q_ref[...] = _head_rms(_dot(u, wq_ref[...]), bd_ref, gq_ref[...], HEAD_DIM)
    k_ref[...] = _head_rms(_dot(u, wk_ref[...]), bd_ref, gk_ref[...], HEAD_DIM)
    v_ref[...] = _dot(u, wv_ref[...])


def _proj_even(h, gm, wz, wx, wdt, wq, wk, wv, dtb, gq, gk, bd, tm):
    n = h.shape[0]

    def tok(c):
        return pl.BlockSpec((tm, c), lambda i: (i, 0))

    widths = (D_MODEL, SSD_CONV_DIM, GATE_LANES, ATT_WIDTH, ATT_WIDTH, ATT_WIDTH)
    return pl.pallas_call(
        _proj_even_kernel,
        grid=(n // tm,),
        in_specs=[tok(D_MODEL), _resident((1, D_MODEL)),
                  _resident(wz.shape), _resident(wx.shape), _resident(wdt.shape),
                  _resident(wq.shape), _resident(wk.shape), _resident(wv.shape),
                  _resident(dtb.shape), _resident(gq.shape), _resident(gk.shape), _resident(bd.shape)],
        out_specs=[tok(c) for c in widths],
        out_shape=[jax.ShapeDtypeStruct((n, c), F32) for c in widths],
        compiler_params=_cparams(1),
        name="proj_even",
    )(h, gm, wz, wx, wdt, wq, wk, wv, dtb, gq, gk, bd)


def _proj_odd_kernel(h_ref, gm_ref, wfq_ref, wfk_ref, wfv_ref, wgt_ref, wmq_ref, wmk_ref, wmv_ref, wmo_ref,
                     gtb_ref, gq_ref, gk_ref, bd_ref,
                     fq_ref, fk_ref, fv_ref, gt_ref, mq_ref, mk_ref, mv_ref, so_ref, *, kv_transposed):
    u = _rms(h_ref[...], gm_ref[...]).astype(BF16)
    fq_ref[...] = _head_rms(_dot(u, wfq_ref[...]), bd_ref, gq_ref[...], HEAD_DIM)
    fk = _head_rms(_dot(u, wfk_ref[...]), bd_ref, gk_ref[...], HEAD_DIM)
    fv = _dot(u, wfv_ref[...])
    fk_ref[...] = fk.T if kv_transposed else fk
    fv_ref[...] = fv.T if kv_transposed else fv
    pre = _dot(u, wgt_ref[...]) + gtb_ref[...]
    lane = lax.broadcasted_iota(jnp.int32, pre.shape, 1)
    is_input_gate = (lane >= FOX_HEADS) & (lane < FOX_HEADS + MLSTM_HEADS)
    gt_ref[...] = jnp.where(is_input_gate, pre, _log_sigmoid(pre))
    mq_ref[...] = _dot(u, wmq_ref[...])
    mk_ref[...] = _dot(u, wmk_ref[...]) * (MLSTM_HEAD_DIM ** -0.5)
    mv_ref[...] = _dot(u, wmv_ref[...])
    so_ref[...] = _sigmoid(_dot(u, wmo_ref[...]))


def _proj_odd(h, gm, wfq, wfk, wfv, wgt, wmq, wmk, wmv, wmo, gtb, gq, gk, bd, tm, seq=None):
    n = h.shape[0]

    def tok(c):
        return pl.BlockSpec((tm, c), lambda i: (i, 0))

    widths = (ATT_WIDTH, ATT_WIDTH, ATT_WIDTH, GATE_LANES, ATT_WIDTH, ATT_WIDTH, ATT_WIDTH, ATT_WIDTH)
    out_specs = [tok(c) for c in widths]
    out_shape = [jax.ShapeDtypeStruct((n, c), F32) for c in widths]
    if seq is not None:
        per_seq = seq // tm
        for o in (1, 2):
            out_specs[o] = pl.BlockSpec((None, ATT_WIDTH, tm), lambda i: (i // per_seq, 0, i % per_seq))
            out_shape[o] = jax.ShapeDtypeStruct((n // seq, ATT_WIDTH, seq), F32)
    ws = (wfq, wfk, wfv, wgt, wmq, wmk, wmv, wmo, gtb, gq, gk, bd)
    return pl.pallas_call(
        functools.partial(_proj_odd_kernel, kv_transposed=seq is not None),
        grid=(n // tm,),
        in_specs=[tok(D_MODEL), _resident((1, D_MODEL))] + [_resident(w.shape) for w in ws],
        out_specs=out_specs,
        out_shape=out_shape,
        compiler_params=_cparams(1),
        name="proj_odd",
    )(h, gm, *ws)


def _tail_kernel(h_ref, a_ref, b_ref, p_ref, woa_ref, wob_ref, g2_ref, wg_ref, wu_ref, wd_ref,
                 wple_ref, gple_ref, ggate_ref, wgate_ref, o_ref, acc_ref):
    h = h_ref[...] + _dot(a_ref[...].astype(BF16), woa_ref[...]) + _dot(b_ref[...].astype(BF16), wob_ref[...])
    h = h + _ffn_half(h, g2_ref, wg_ref, wu_ref, wd_ref, acc_ref)
    e = _rms(_dot(p_ref[...].astype(BF16), wple_ref[...]), gple_ref[...])
    gate = _sigmoid(_dot(_rms(h, ggate_ref[...]).astype(BF16), wgate_ref[...]))
    o_ref[...] = h + e * gate


def _tail(h, a, b, p, woa, wob, g2, wg, wu, wd, wple, gple, ggate, wgate, tm):
    n = h.shape[0]

    def tok(c):
        return pl.BlockSpec((tm, c), lambda i: (i, 0))

    ws = (woa, wob, g2, wg, wu, wd, wple, gple, ggate, wgate)
    return pl.pallas_call(
        _tail_kernel,
        grid=(n // tm,),
        in_specs=[tok(D_MODEL), tok(a.shape[1]), tok(b.shape[1]), tok(PLE_DIM)] + [_resident(w.shape) for w in ws],
        out_specs=tok(D_MODEL),
        out_shape=jax.ShapeDtypeStruct((n, D_MODEL), F32),
        scratch_shapes=[pltpu.VMEM((tm, D_MODEL), F32)],
        compiler_params=_cparams(1),
        name="tail",
    )(h, a, b, p, *ws)


CONV_PAD = 8
SSD_PAIR = 128
SSD_GROUP_COLS = 512
SSD_STATE = 64


def _ssd_kernel(xbc_ref, z_ref, dt_ref, dtT_ref, conv0_ref, st0_ref, cw_ref, cb_ref,
                alog_ref, alogT_ref, dskip_ref, gn_ref, e_ref,
                y_ref, convo_ref, sto_ref, xp_ref, st_ref, *, tb, t):
    i = pl.program_id(1)

    @pl.when(i == 0)
    def _():
        xp_ref[0:CONV_PAD, :] = conv0_ref[...]
        st_ref[...] = jnp.zeros_like(st_ref)
        st_ref[0:SSD_STATE, 0:SSD_GROUP_COLS] = st0_ref[0]
        st_ref[SSD_STATE:2 * SSD_STATE, SSD_GROUP_COLS:2 * SSD_GROUP_COLS] = st0_ref[1]

    xp_ref[CONV_PAD:CONV_PAD + tb, :] = xbc_ref[...]
    cw = cw_ref[...]
    pre = cb_ref[...]
    for j in range(SSD_CONV):
        off = CONV_PAD - (SSD_CONV - 1) + j
        pre = pre + cw[j:j + 1, :] * xp_ref[off:off + tb, :]
    xc = _silu(pre)
    tail_rows = xp_ref[tb:tb + CONV_PAD, :]
    convo_ref[...] = tail_rows
    xp_ref[0:CONV_PAD, :] = tail_rows

    tri = _lower_tri(t)
    ltri = tri.astype(BF16)
    utri = (lax.broadcasted_iota(jnp.int32, (t, t), 0) <= lax.broadcasted_iota(jnp.int32, (t, t), 1)).astype(BF16)
    a_row = -jnp.exp(alog_ref[...])
    a_col = -jnp.exp(alogT_ref[...])
    e01 = e_ref[...]
    lane = lax.broadcasted_iota(jnp.int32, (t, SSD_PAIR), 1)
    st_r = lax.broadcasted_iota(jnp.int32, st_ref.shape, 0)
    st_c = lax.broadcasted_iota(jnp.int32, st_ref.shape, 1)
    st_own = (st_r < SSD_STATE) == (st_c < SSD_GROUP_COLS)

    for c in range(tb // t):
        r0 = c * t
        xs = xc[r0:r0 + t, 0:SSD_D_INNER]
        b2 = xc[r0:r0 + t, SSD_D_INNER:SSD_D_INNER + 128]
        c2 = xc[r0:r0 + t, SSD_D_INNER + 128:SSD_D_INNER + 256]
        dtc = dt_ref[r0:r0 + t, :]
        acum = _dot3_r(ltri, dtc * a_row)
        acum_t = _dot3_l(dtT_ref[:, r0:r0 + t] * a_col, utri)
        acum_x = _dot3_l(acum, e01)
        xdt = xs * _dot3_l(dtc, e01)
        last = acum_x[t - 1:t, :]
        tailx = jnp.exp(last - acum_x) * xdt
        b2b = b2.astype(BF16)
        y = _dot(c2.astype(BF16), st_ref[...].astype(BF16)) * jnp.exp(acum_x) + dskip_ref[...] * xs
        cbs = []
        for g in range(2):
            in_g = (lane >= HEAD_DIM * g) & (lane < HEAD_DIM * (g + 1))
            cbs.append(_dot_nt(jnp.where(in_g, c2, 0.0).astype(BF16), b2b))
        ys = []
        for j in range(SSD_HEADS // 2):
            xpair = xdt[:, SSD_PAIR * j:SSD_PAIR * (j + 1)]
            acc = None
            for s in range(2):
                hd = 2 * j + s
                seg = acum[:, hd:hd + 1] - acum_t[hd:hd + 1, :]
                m = jnp.exp(jnp.where(tri, seg, NEG_INF)) * cbs[j // 4]
                xh = jnp.where((lane < HEAD_DIM) if s == 0 else (lane >= HEAD_DIM), xpair, 0.0)
                d = _dot(m.astype(BF16), xh.astype(BF16))
                acc = d if acc is None else acc + d
            ys.append(acc)
        y = y + jnp.concatenate(ys, axis=1)
        upd = _dot_tn(b2b, tailx.astype(BF16))
        st_ref[...] = jnp.where(st_own, st_ref[...] * jnp.exp(last) + upd, 0.0)
        y_ref[r0:r0 + t, :] = _rms(y * _silu(z_ref[r0:r0 + t, :]), gn_ref[...])

    sto_ref[0] = st_ref[0:SSD_STATE, 0:SSD_GROUP_COLS]
    sto_ref[1] = st_ref[SSD_STATE:2 * SSD_STATE, SSD_GROUP_COLS:2 * SSD_GROUP_COLS]


def _ssd(xbc, z, dt, dt_t, conv0, st0, cw, cb, alog, alog_t, dskip, gn, e01, tb, t):
    bsz, n, _ = xbc.shape
    consts = (cw, cb, alog, alog_t, dskip, gn, e01)
    return pl.pallas_call(
        functools.partial(_ssd_kernel, tb=tb, t=t),
        grid=(bsz, n // tb),
        in_specs=[pl.BlockSpec((None, tb, SSD_CONV_DIM), lambda b, i: (b, i, 0)),
                  pl.BlockSpec((None, tb, SSD_D_INNER), lambda b, i: (b, i, 0)),
                  pl.BlockSpec((None, tb, GATE_LANES), lambda b, i: (b, i, 0)),
                  pl.BlockSpec((None, SSD_HEADS, tb), lambda b, i: (b, 0, i)),
                  pl.BlockSpec((None, CONV_PAD, SSD_CONV_DIM), lambda b, i: (b, 0, 0)),
                  pl.BlockSpec((None, 2, SSD_STATE, SSD_GROUP_COLS), lambda b, i: (b, 0, 0, 0))]
                 + [_resident(w.shape) for w in consts],
        out_specs=[pl.BlockSpec((None, tb, SSD_D_INNER), lambda b, i: (b, i, 0)),
                   pl.BlockSpec((None, CONV_PAD, SSD_CONV_DIM), lambda b, i: (b, 0, 0)),
                   pl.BlockSpec((None, 2, SSD_STATE, SSD_GROUP_COLS), lambda b, i: (b, 0, 0, 0))],
        out_shape=[jax.ShapeDtypeStruct((bsz, n, SSD_D_INNER), F32),
                   jax.ShapeDtypeStruct((bsz, CONV_PAD, SSD_CONV_DIM), F32),
                   jax.ShapeDtypeStruct((bsz, 2, SSD_STATE, SSD_GROUP_COLS), F32)],
        scratch_shapes=[pltpu.VMEM((CONV_PAD + tb, SSD_CONV_DIM), F32),
                        pltpu.VMEM((2 * SSD_STATE, 2 * SSD_GROUP_COLS), F32)],
        compiler_params=_cparams(2),
        name="ssd",
    )(xbc, z, dt, dt_t, conv0, st0, *consts)


def _head_expand(n_heads, width):
    e = np.zeros((GATE_LANES, n_heads * width), np.float32)
    for h in range(n_heads):
        e[h, h * width:(h + 1) * width] = 1.0
    return jnp.asarray(e, BF16)


def _ssd_mixer(xbc, z, dt, conv_buf, s0, conv_w, conv_b, a_log, d_skip, ssd_norm, tb, t):
    bsz = xbc.shape[0]
    dt_t = jnp.swapaxes(dt[..., :SSD_HEADS], 1, 2)
    conv0 = jnp.pad(conv_buf, ((0, 0), (CONV_PAD - (SSD_CONV - 1), 0), (0, 0)))
    st0 = s0.reshape(bsz, 2, 8, HEAD_DIM, SSD_STATE).transpose(0, 1, 4, 2, 3).reshape(bsz, 2, SSD_STATE, SSD_GROUP_COLS)
    cw = jnp.pad(conv_w, ((0, CONV_PAD - SSD_CONV), (0, 0)))
    alog = jnp.pad(a_log, (0, GATE_LANES - SSD_HEADS)).reshape(1, GATE_LANES)
    y, convo, sto = _ssd(xbc, z, dt, dt_t, conv0, st0, cw, conv_b.reshape(1, -1), alog, a_log.reshape(SSD_HEADS, 1),
                         jnp.repeat(d_skip, HEAD_DIM).reshape(1, -1), ssd_norm.reshape(1, -1),
                         _head_expand(SSD_HEADS, HEAD_DIM), tb, t)
    s_new = sto.reshape(bsz, 2, SSD_STATE, 8, HEAD_DIM).transpose(0, 1, 3, 4, 2).reshape(bsz, SSD_HEADS, HEAD_DIM, SSD_STATE)
    return y, convo[:, CONV_PAD - (SSD_CONV - 1):], s_new


GATE_I = FOX_HEADS
GATE_F = FOX_HEADS + MLSTM_HEADS


def _mlstm_kernel(q_ref, k_ref, v_ref, so_ref, gt_ref, gtT_ref, c0_ref, n0_ref, m0_ref, gn_ref,
                  h_ref, co_ref, no_ref, mo_ref, c_ref, n_ref, m_ref, *, t):
    i = pl.program_id(1)

    @pl.when(i == 0)
    def _():
        c_ref[...] = c0_ref[...]
        n_ref[...] = n0_ref[...]
        m_ref[...] = m0_ref[...]

    tri = _lower_tri(t)
    ltri = tri.astype(BF16)
    utri = (lax.broadcasted_iota(jnp.int32, (t, t), 0) <= lax.broadcasted_iota(jnp.int32, (t, t), 1)).astype(BF16)
    gt = gt_ref[...]
    gt_t = gtT_ref[...]
    cum = _dot3_r(ltri, gt)
    cum_t = _dot3_l(gt_t, utri)
    m_vec = m_ref[...]
    m_lane = lax.broadcasted_iota(jnp.int32, m_vec.shape, 1)
    m_next = m_vec
    for hh in range(MLSTM_HEADS):
        sl = slice(MLSTM_HEAD_DIM * hh, MLSTM_HEAD_DIM * (hh + 1))
        bcol = cum[:, GATE_F + hh:GATE_F + hh + 1]
        brow = cum_t[GATE_F + hh:GATE_F + hh + 1, :]
        ig_row = gt_t[GATE_I + hh:GATE_I + hh + 1, :]
        ig_col = gt[:, GATE_I + hh:GATE_I + hh + 1]
        dmat = jnp.where(tri, bcol - brow + ig_row, NEG_INF)
        g = bcol + m_vec[:, hh:hh + 1]
        m_t = jnp.maximum(g, jnp.max(dmat, axis=-1, keepdims=True))
        w = jnp.exp(dmat - m_t)
        inter = jnp.exp(g - m_t)
        qh = q_ref[:, sl]
        kh = k_ref[:, sl]
        vh = v_ref[:, sl]
        qb = qh.astype(BF16)
        kb = kh.astype(BF16)
        a = w * _dot_nt(qb, kb)
        cm = c_ref[hh]
        nv = n_ref[hh:hh + 1, :]
        num = inter * _dot(qb, cm.astype(BF16)) + _dot(a.astype(BF16), vh.astype(BF16))
        den = inter * jnp.sum(qh * nv, axis=-1, keepdims=True) + jnp.sum(a, axis=-1, keepdims=True)
        hout = num / jnp.maximum(jnp.abs(den), jnp.exp(-m_t))
        m_last = m_t[t - 1:t, :]
        inter_last = inter[t - 1:t, :]
        w_end = jnp.exp(bcol[t - 1:t, :] - bcol + ig_col - m_last)
        c_ref[hh] = inter_last * cm + _dot_tn(kb, (w_end * vh).astype(BF16))
        n_ref[hh:hh + 1, :] = inter_last * nv + jnp.sum(w_end * kh, axis=0, keepdims=True)
        m_next = jnp.where(m_lane == hh, m_last, m_next)
        h_ref[:, sl] = _rms(hout, gn_ref[:, sl]) * so_ref[:, sl]
    m_ref[...] = m_next
    co_ref[...] = c_ref[...]
    no_ref[...] = n_ref[...]
    mo_ref[...] = m_next


def _mlstm(q, k, v, so, gt, gt_t, c0, n0, m0, gn, t):
    bsz, n, _ = q.shape
    tok = pl.BlockSpec((None, t, ATT_WIDTH), lambda b, i: (b, i, 0))
    c_spec = pl.BlockSpec((None, MLSTM_HEADS, MLSTM_HEAD_DIM, MLSTM_HEAD_DIM), lambda b, i: (b, 0, 0, 0))
    n_spec = pl.BlockSpec((None, MLSTM_HEADS, MLSTM_HEAD_DIM), lambda b, i: (b, 0, 0))
    m_spec = pl.BlockSpec((None, 1, GATE_LANES), lambda b, i: (b, 0, 0))
    return pl.pallas_call(
        functools.partial(_mlstm_kernel, t=t),
        grid=(bsz, n // t),
        in_specs=[tok, tok, tok, tok,
                  pl.BlockSpec((None, t, GATE_LANES), lambda b, i: (b, i, 0)),
                  pl.BlockSpec((None, 16, t), lambda b, i: (b, 0, i)),
                  c_spec, n_spec, m_spec, _resident(gn.shape)],
        out_specs=[tok, c_spec, n_spec, m_spec],
        out_shape=[jax.ShapeDtypeStruct((bsz, n, ATT_WIDTH), F32),
                   jax.ShapeDtypeStruct(c0.shape, F32),
                   jax.ShapeDtypeStruct(n0.shape, F32),
                   jax.ShapeDtypeStruct(m0.shape, F32)],
        scratch_shapes=[pltpu.VMEM((MLSTM_HEADS, MLSTM_HEAD_DIM, MLSTM_HEAD_DIM), F32),
                        pltpu.VMEM((MLSTM_HEADS, MLSTM_HEAD_DIM), F32),
                        pltpu.VMEM((1, GATE_LANES), F32)],
        compiler_params=_cparams(2),
        name="mlstm",
    )(q, k, v, so, gt, gt_t, c0, n0, m0, gn)


def _mlstm_mixer(mq, mk, mv, so, gates, c0, n0, m0, ml_norm, t):
    gt_t = jnp.swapaxes(gates[..., :16], 1, 2)
    m0p = jnp.pad(m0, ((0, 0), (0, GATE_LANES - MLSTM_HEADS)))[:, None, :]
    h, c_new, n_new, m_new = _mlstm(mq, mk, mv, so, gates, gt_t, c0, n0, m0p, ml_norm.reshape(1, -1), t)
    return h, c_new, n_new, m_new[:, 0, :MLSTM_HEADS]


PAIR = 2 * HEAD_DIM
N_PAIRS = ATT_WIDTH // PAIR
QK_SCALE = HEAD_DIM ** -0.5


def _in_head(lane, h):
    return (lane >= HEAD_DIM * h) & (lane < HEAD_DIM * (h + 1))


def _stack_heads(q):
    lane = lax.broadcasted_iota(jnp.int32, q.shape, 1)
    return jnp.concatenate([jnp.where(_in_head(lane, h), q, 0.0) for h in range(ATT_WIDTH // HEAD_DIM)], axis=0)


def _unstack_heads(o, nq):
    lane = lax.broadcasted_iota(jnp.int32, (nq, ATT_WIDTH), 1)
    out = o[0:nq, :]
    for h in range(1, ATT_WIDTH // HEAD_DIM):
        out = jnp.where(_in_head(lane, h), o[h * nq:(h + 1) * nq, :], out)
    return out


def _rows_per_head(x_t, nq):
    n = x_t.shape[1]
    return jnp.concatenate([jnp.broadcast_to(x_t[h:h + 1, :], (nq, n)) for h in range(x_t.shape[0])], axis=0)


BAND_QB = BAND_PAST // 2
BAND_STEP_HEADS = 4


def _band_prompt_kernel(q_ref, k2_ref, k1_ref, k0_ref, v2_ref, v1_ref, v0_ref, bias_ref, o_ref):
    i = pl.program_id(2)
    qb, width = q_ref.shape
    lane = lax.broadcasted_iota(jnp.int32, (qb, width), 1)
    q = q_ref[...] * QK_SCALE
    kcat = jnp.concatenate([k2_ref[...], k1_ref[...], k0_ref[...]], axis=0).astype(BF16)
    vcat = jnp.concatenate([v2_ref[...], v1_ref[...], v0_ref[...]], axis=0).astype(BF16)
    col = lax.broadcasted_iota(jnp.int32, (qb, 3 * qb), 1)
    in_sequence = col >= (2 - i) * qb
    out = None
    for s in range(width // HEAD_DIM):
        qh = jnp.where(_in_head(lane, s), q, 0.0).astype(BF16)
        sc = jnp.where(in_sequence, _dot_nt(qh, kcat) + bias_ref[s], NEG_INF)
        p = jnp.exp(sc - jnp.max(sc, axis=-1, keepdims=True))
        o = _dot(p.astype(BF16), vcat) / jnp.sum(p, axis=-1, keepdims=True)
        out = o if out is None else jnp.where(_in_head(lane, s), o, out)
    o_ref[...] = out


def _band_prompt(q, k, v, bias):
    bsz, n, _ = q.shape
    qb = BAND_QB

    width = BAND_STEP_HEADS * HEAD_DIM

    def kv_spec(back):
        return pl.BlockSpec((None, qb, width), lambda b, j, i: (b, jnp.maximum(i - back, 0), j))

    return pl.pallas_call(
        _band_prompt_kernel,
        grid=(bsz, ATT_WIDTH // width, n // qb),
        in_specs=[kv_spec(0), kv_spec(2), kv_spec(1), kv_spec(0), kv_spec(2), kv_spec(1), kv_spec(0),
                  pl.BlockSpec((BAND_STEP_HEADS, qb, 3 * qb), lambda b, j, i: (j, 0, 0))],
        out_specs=kv_spec(0),
        out_shape=jax.ShapeDtypeStruct((bsz, n, ATT_WIDTH), F32),
        compiler_params=_cparams(3),
        name="band_prompt",
    )(q, k, k, k, v, v, v, bias)


def _rel_bias_block(rel_table, rows, cols, offset):
    period = rows + cols - 1
    dist = offset + rows - 1 - np.arange(period)
    profile = rel_table[:, np.clip(dist, -REL_CLIP, REL_CLIP) + REL_CLIP].astype(F32)
    rolled = jnp.roll(profile, -(rows - 1), axis=1)
    flat = jnp.tile(rolled, (1, rows))[:, :rows * (period - 1)]
    return flat.reshape(-1, rows, period - 1)[:, :, :cols]


def _band_prompt_bias(rel_table):
    qb = BAND_QB
    r = np.arange(qb)[:, None] + 2 * qb
    c = np.arange(3 * qb)[None, :]
    in_band = (c // CHUNK <= r // CHUNK) & (c // CHUNK >= r // CHUNK - BAND_PAST // CHUNK)
    return jnp.where(jnp.asarray(in_band)[None], _rel_bias_block(rel_table, qb, 3 * qb, 2 * qb), NEG_INF)


def _band_sample_kernel(q_ref, kct_ref, vct_ref, kn_ref, vn_ref, bc_ref, bn_ref, o_ref):
    nq = q_ref.shape[0]
    qs = _stack_heads(q_ref[...] * QK_SCALE).astype(BF16)
    s_c = _dot(qs, kct_ref[...].astype(BF16)) + bc_ref[...]
    s_n = _dot_nt(qs, kn_ref[...].astype(BF16)) + bn_ref[...]
    m = jnp.maximum(jnp.max(s_c, axis=-1, keepdims=True), jnp.max(s_n, axis=-1, keepdims=True))
    p_c = jnp.exp(s_c - m)
    p_n = jnp.exp(s_n - m)
    l = jnp.sum(p_c, axis=-1, keepdims=True) + jnp.sum(p_n, axis=-1, keepdims=True)
    o = _dot_nt(p_c.astype(BF16), vct_ref[...].astype(BF16)) + _dot(p_n.astype(BF16), vn_ref[...].astype(BF16))
    o_ref[...] = _unstack_heads(o / l, nq)


def _band_sample(q, k, v, kt_cache, vt_cache, bias_c, bias_n):
    bsz, nq, _ = q.shape
    nb = kt_cache.shape[2]
    new = pl.BlockSpec((None, nq, ATT_WIDTH), lambda b: (b, 0, 0))
    old = pl.BlockSpec((None, ATT_WIDTH, nb), lambda b: (b, 0, 0))
    return pl.pallas_call(
        _band_sample_kernel,
        grid=(bsz,),
        in_specs=[new, old, old, new, new, _resident(bias_c.shape), _resident(bias_n.shape)],
        out_specs=new,
        out_shape=jax.ShapeDtypeStruct((bsz, nq, ATT_WIDTH), F32),
        compiler_params=_cparams(1),
        name="band_sample",
    )(q, kt_cache, vt_cache, k, v, bias_c, bias_n)


def _band_sample_bias(rel_table, nq, nb):
    bias = _rel_bias_block(rel_table, nq, nb + nq, nb).reshape(BAND_HEADS * nq, nb + nq)
    return bias[:, :nb], bias[:, nb:]


def _cumsum_lanes_kernel(x_ref, o_ref, carry_ref):
    lb = x_ref.shape[1]

    @pl.when(pl.program_id(0) == 0)
    def _():
        carry_ref[...] = jnp.zeros_like(carry_ref)

    utri = (lax.broadcasted_iota(jnp.int32, (lb, lb), 0) <= lax.broadcasted_iota(jnp.int32, (lb, lb), 1)).astype(BF16)
    c = _dot3_l(x_ref[...], utri) + carry_ref[...]
    o_ref[...] = c
    carry_ref[...] = c[:, lb - 1:lb]


def _cumsum_lanes(x, lb):
    rows, n = x.shape
    blk = pl.BlockSpec((rows, lb), lambda i: (0, i))
    return pl.pallas_call(
        _cumsum_lanes_kernel,
        grid=(n // lb,),
        in_specs=[blk],
        out_specs=blk,
        out_shape=jax.ShapeDtypeStruct((rows, n), F32),
        scratch_shapes=[pltpu.VMEM((rows, 1), F32)],
        compiler_params=_cparams(1),
        name="cumsum_lanes",
    )(x)


FOX_BLK = 512
FOX_UNROLL = 4


LOG2E = 1.4426950408889634


def _fox_prompt_kernel(q_ref, kt_ref, vt_ref, cq_ref, ck_ref, o_ref, m_ref, acc_ref):
    i = pl.program_id(2)
    blk = q_ref.shape[0]
    lane = lax.broadcasted_iota(jnp.int32, (blk, PAIR), 1)
    q = q_ref[...] * (QK_SCALE * LOG2E)
    q2 = jnp.concatenate([jnp.where(_in_head(lane, s), q, 0.0) for s in range(2)], axis=0).astype(BF16)
    cq = cq_ref[...] * LOG2E
    cqb = jnp.concatenate([jnp.broadcast_to(cq[:, s:s + 1], (blk, PAIR)) for s in range(2)], axis=0)
    tri = _lower_tri(blk)
    tri2 = jnp.concatenate([tri, tri], axis=0)
    ones = jnp.ones((PAIR, blk), BF16)
    m_ref[...] = jnp.full_like(m_ref, NEG_INF)
    acc_ref[...] = jnp.zeros_like(acc_ref)

    def step(kk, on_diagonal):
        k0 = pl.multiple_of(kk * blk, blk)
        kb = kt_ref[:, pl.ds(k0, blk)].astype(BF16)
        vb = jnp.concatenate([vt_ref[:, pl.ds(k0, blk)].astype(BF16), ones], axis=0)
        ck = ck_ref[:, pl.ds(k0, blk)] * LOG2E
        sc = _dot(q2, kb)
        sc = jnp.concatenate([sc[:blk] - ck[0:1], sc[blk:] - ck[1:2]], axis=0)
        if on_diagonal:
            sc = jnp.where(tri2, sc, NEG_INF)
        chunks = [sc[:, PAIR * c:PAIR * (c + 1)] for c in range(blk // PAIR)]
        rm = chunks[0]
        for ch in chunks[1:]:
            rm = jnp.maximum(rm, ch)
        rmb = jnp.broadcast_to(jnp.max(rm, axis=-1, keepdims=True), (2 * blk, PAIR))
        m_old = m_ref[...]
        m_new = jnp.maximum(m_old, rmb + cqb)
        shift = m_new - cqb
        p = jnp.concatenate([jnp.exp2(ch - shift) for ch in chunks], axis=1).astype(BF16)
        alpha = jnp.exp2(m_old - m_new)
        acc_ref[...] = jnp.concatenate([alpha, alpha], axis=1) * acc_ref[...] + _dot_nt(p, vb)
        m_ref[...] = m_new

    def body(quad, carry):
        for u in range(FOX_UNROLL):
            step(FOX_UNROLL * quad + u, False)
        return carry

    lax.fori_loop(0, lax.shift_right_logical(i, FOX_UNROLL.bit_length() - 1), body, 0)
    done = jnp.bitwise_and(i, -FOX_UNROLL)
    rem = jnp.bitwise_and(i, FOX_UNROLL - 1)
    for r in range(FOX_UNROLL):
        @pl.when(rem == r)
        def _(r=r):
            for u in range(r):
                step(done + u, False)
            step(i, True)
    acc = acc_ref[...]
    o = acc[:, :PAIR] / acc[:, PAIR:]
    o_ref[...] = jnp.where(lane < HEAD_DIM, o[:blk], o[blk:])


def _fox_prompt(q, kt, vt, ct_col, ct_row):
    bsz, n, _ = q.shape
    blk = FOX_BLK
    tile = pl.BlockSpec((None, blk, PAIR), lambda b, j, i: (b, i, j))
    whole = pl.BlockSpec((None, PAIR, n), lambda b, j, i: (b, j, 0))
    return pl.pallas_call(
        _fox_prompt_kernel,
        grid=(bsz, N_PAIRS, n // blk),
        in_specs=[tile, whole, whole,
                  pl.BlockSpec((None, None, blk, 2), lambda b, j, i: (b, j, i, 0)),
                  pl.BlockSpec((None, None, 2, n), lambda b, j, i: (b, j, 0, 0))],
        out_specs=tile,
        out_shape=jax.ShapeDtypeStruct((bsz, n, ATT_WIDTH), F32),
        scratch_shapes=[pltpu.VMEM((2 * blk, PAIR), F32), pltpu.VMEM((2 * blk, 2 * PAIR), F32)],
        compiler_params=_cparams(3),
        name="fox_prompt",
    )(q, kt, vt, ct_col, ct_row)


FOX_KV_BLK = 2048


def _fox_sample_kernel(q_ref, kct_ref, vct_ref, kn_ref, vn_ref, cq_ref, ckc_ref, ckn_ref, o_ref, m_ref, l_ref, acc_ref):
    j = pl.program_id(1)
    nq = q_ref.shape[0]

    @pl.when(j == 0)
    def _():
        m_ref[...] = jnp.full_like(m_ref, NEG_INF)
        l_ref[...] = jnp.zeros_like(l_ref)
        acc_ref[...] = jnp.zeros_like(acc_ref)

    qs = _stack_heads(q_ref[...] * QK_SCALE).astype(BF16)
    cq = cq_ref[...]

    def absorb(sc, pv):
        m_old = m_ref[...]
        m_new = jnp.maximum(m_old, jnp.max(sc, axis=-1, keepdims=True))
        alpha = jnp.exp(m_old - m_new)
        p = jnp.exp(sc - m_new)
        l_ref[...] = alpha * l_ref[...] + jnp.sum(p, axis=-1, keepdims=True)
        acc_ref[...] = alpha * acc_ref[...] + pv(p.astype(BF16))
        m_ref[...] = m_new

    vct = vct_ref[...].astype(BF16)
    absorb(_dot(qs, kct_ref[...].astype(BF16)) + (cq - _rows_per_head(ckc_ref[...], nq)), lambda p: _dot_nt(p, vct))

    @pl.when(j == pl.num_programs(1) - 1)
    def _():
        sc = _dot_nt(qs, kn_ref[...].astype(BF16)) + (cq - _rows_per_head(ckn_ref[...], nq))
        row = lax.broadcasted_iota(jnp.int32, sc.shape, 0)
        col = lax.broadcasted_iota(jnp.int32, sc.shape, 1)
        vn = vn_ref[...].astype(BF16)
        absorb(jnp.where(col <= row % nq, sc, NEG_INF), lambda p: _dot(p, vn))
        o_ref[...] = _unstack_heads(acc_ref[...] / l_ref[...], nq)


def _fox_sample(q, k, v, kt_cache, vt_cache, cq, ck_cache, ck_new):
    bsz, nq, _ = q.shape
    npast = kt_cache.shape[2]
    kvb = FOX_KV_BLK
    new = pl.BlockSpec((None, nq, ATT_WIDTH), lambda b, j: (b, 0, 0))
    old = pl.BlockSpec((None, ATT_WIDTH, kvb), lambda b, j: (b, 0, j))
    rows = FOX_HEADS * nq
    return pl.pallas_call(
        _fox_sample_kernel,
        grid=(bsz, npast // kvb),
        in_specs=[new, old, old, new, new,
                  pl.BlockSpec((None, rows, 1), lambda b, j: (b, 0, 0)),
                  pl.BlockSpec((None, FOX_HEADS, kvb), lambda b, j: (b, 0, j)),
                  pl.BlockSpec((None, FOX_HEADS, nq), lambda b, j: (b, 0, 0))],
        out_specs=new,
        out_shape=jax.ShapeDtypeStruct((bsz, nq, ATT_WIDTH), F32),
        scratch_shapes=[pltpu.VMEM((rows, 1), F32), pltpu.VMEM((rows, 1), F32), pltpu.VMEM((rows, ATT_WIDTH), F32)],
        compiler_params=_cparams(2),
        name="fox_sample",
    )(q, kt_cache, vt_cache, k, v, cq, ck_cache, ck_new)


LANES = 128


def _fox_prompt_mixer(fq, fk, fv, gates):
    bsz, n, _ = fq.shape
    logf_t = jnp.swapaxes(gates[..., :FOX_HEADS], 1, 2).reshape(bsz * FOX_HEADS, n)
    ct = _cumsum_lanes(logf_t, FOX_BLK).reshape(bsz, N_PAIRS, 2, n)
    return _fox_prompt(fq, fk, fv, jnp.swapaxes(ct, 2, 3), ct)


def _cache_t(cache):
    bsz, npast = cache.shape[:2]
    return jnp.transpose(cache, (0, 2, 3, 1)).reshape(bsz, ATT_WIDTH, npast)


def _fox_sample_mixer(fq, fk, fv, gates, k_cache, v_cache, lf_cache):
    bsz, nq, _ = fq.shape
    npast = k_cache.shape[1]
    lf_all = jnp.concatenate([jnp.swapaxes(lf_cache, 1, 2), jnp.swapaxes(gates[..., :FOX_HEADS], 1, 2)], axis=2)
    total = npast + nq
    padded = -(-total // LANES) * LANES
    lb = next(c for c in (512, 384, 256, 128) if padded % c == 0)
    lf_all = jnp.pad(lf_all, ((0, 0), (0, 0), (0, padded - total))).reshape(bsz * FOX_HEADS, padded)
    ct = _cumsum_lanes(lf_all, lb).reshape(bsz, FOX_HEADS, padded)
    ck_new = ct[:, :, npast:total]
    return _fox_sample(fq, fk, fv, _cache_t(k_cache), _cache_t(v_cache), ck_new.reshape(bsz, FOX_HEADS * nq, 1),
                       ct[:, :, :npast], ck_new)


TM_FFN = 1024
TM_PROJ = 512
SSD_PROMPT_CHUNK = 128
MLSTM_PROMPT_CHUNK = 256
SSD_BLOCK_ROWS = 512


def _block_diag_ones(n, width):
    idx = np.arange(n) // width
    return jnp.asarray(idx[:, None] == idx[None, :], BF16)


def _cols(w, splits):
    cuts = np.concatenate([[0], np.cumsum(splits)])
    return [w[:, int(cuts[i]):int(cuts[i + 1])] for i in range(len(splits))]


def _pad_cols(w, width):
    return jnp.pad(w, ((0, 0), (0, width - w.shape[1])))


def _row(v, width=None):
    v = v.reshape(1, -1).astype(F32)
    return v if width is None else _pad_cols(v, width)


def kernel(x_prompt, x_sample, p_prompt, p_sample, state_ssd_conv, state_ssd, cache_band_k, cache_band_v, cache_fox_k, cache_fox_v, cache_fox_logf, state_mlstm_C, state_mlstm_n, state_mlstm_m, norm_ffn1, ffn1_wg, ffn1_wu, ffn1_wd, norm_mix, norm_ffn2, ffn2_wg, ffn2_wu, ffn2_wd, ple_proj, ple_norm, ple_gate_norm, ple_gate_w, even_w_in, even_w_out, ssd_conv_w, ssd_conv_b, ssd_a_log, ssd_dt_bias, ssd_d, ssd_norm, band_q_norm, band_k_norm, band_rel_bias, odd_w_in, odd_w_out, fox_q_norm, fox_k_norm, fox_f_bias, mlstm_i_bias, mlstm_f_bias, mlstm_norm):
    bp, sp, _ = x_prompt.shape
    bs, ss, _ = x_sample.shape
    depth = norm_ffn1.shape[0]
    n_heads = ATT_WIDTH // HEAD_DIM
    assert ss <= CHUNK and sp % max(SSD_BLOCK_ROWS, FOX_BLK, TM_FFN) == 0 and (bs * ss) % TM_PROJ == 0

    def bf(w):
        return w.astype(BF16)

    bd = _block_diag_ones(ATT_WIDTH, HEAD_DIM)
    hp = x_prompt.reshape(bp * sp, D_MODEL)
    hs = x_sample.reshape(bs * ss, D_MODEL)
    tm_s = min(TM_FFN, bs * ss)
    even_p, even_s, odd_p, odd_s = [], [], [], []

    for i in range(depth):
        j = i // 2
        w1 = (_row(norm_ffn1[i]), bf(ffn1_wg[i]), bf(ffn1_wu[i]), bf(ffn1_wd[i]))
        hp = _ffn1(hp, *w1, tm=TM_FFN)
        hs = _ffn1(hs, *w1, tm=tm_s)
        gm = _row(norm_mix[i])

        if i % 2 == 0:
            wz, wx, wdt, wq, wk, wv = _cols(even_w_in[j], EVEN_SPLITS)
            wproj = (bf(wz), bf(wx), bf(_pad_cols(wdt, GATE_LANES)), bf(wq), bf(wk), bf(wv),
                     _row(ssd_dt_bias[j], GATE_LANES), _row(jnp.tile(band_q_norm[j], n_heads)),
                     _row(jnp.tile(band_k_norm[j], n_heads)), bd)
            ssd_w = (ssd_conv_w[j], ssd_conv_b[j], ssd_a_log[j], ssd_d[j], ssd_norm[j])

            z, xbc, dt, q, k, v = [a.reshape(bp, sp, -1) for a in _proj_even(hp, gm, *wproj, tm=TM_PROJ)]
            ya_p, conv_p, st_p = _ssd_mixer(xbc, z, dt, jnp.zeros((bp, SSD_CONV - 1, SSD_CONV_DIM), F32),
                                            jnp.zeros((bp, SSD_HEADS, HEAD_DIM, SSD_STATE), F32), *ssd_w,
                                            tb=SSD_BLOCK_ROWS, t=SSD_PROMPT_CHUNK)
            ob_p = _band_prompt(q, k, v, _band_prompt_bias(band_rel_bias[j]))
            keep = min(BAND_PAST, sp)
            even_p.append((conv_p, st_p, k[:, sp - keep:].reshape(bp, keep, n_heads, HEAD_DIM),
                           v[:, sp - keep:].reshape(bp, keep, n_heads, HEAD_DIM)))

            z, xbc, dt, q, k, v = [a.reshape(bs, ss, -1) for a in _proj_even(hs, gm, *wproj, tm=TM_PROJ)]
            ya_s, conv_s, st_s = _ssd_mixer(xbc, z, dt, state_ssd_conv[j], state_ssd[j], *ssd_w, tb=ss, t=ss)
            nb = cache_band_k.shape[2]
            ob_s = _band_sample(q, k, v, _cache_t(cache_band_k[j]), _cache_t(cache_band_v[j]),
                                *_band_sample_bias(band_rel_bias[j], ss, nb))
            even_s.append((conv_s, st_s, k.reshape(bs, ss, n_heads, HEAD_DIM), v.reshape(bs, ss, n_heads, HEAD_DIM)))

            mix_p, mix_s = (ya_p, ob_p), (ya_s, ob_s)
            w_out = even_w_out[j]
        else:
            wfq, wfk, wfv, wff, wmq, wmk, wmv, wmi, wmf, wmo = _cols(odd_w_in[j], ODD_SPLITS)
            wgt = _pad_cols(jnp.concatenate([wff, wmi, wmf], axis=1), GATE_LANES)
            gtb = _row(jnp.concatenate([fox_f_bias[j], mlstm_i_bias[j], mlstm_f_bias[j]]), GATE_LANES)
            wproj = (bf(wfq), bf(wfk), bf(wfv), bf(wgt), bf(wmq), bf(wmk), bf(wmv), bf(wmo), gtb,
                     _row(jnp.tile(fox_q_norm[j], n_heads)), _row(jnp.tile(fox_k_norm[j], n_heads)), bd)

            fq, fkt, fvt, gt, mq, mk, mv, so = _proj_odd(hp, gm, *wproj, tm=TM_PROJ, seq=sp)
            fq, gt, mq, mk, mv, so = [a.reshape(bp, sp, -1) for a in (fq, gt, mq, mk, mv, so)]
            oc_p = _fox_prompt_mixer(fq, fkt, fvt, gt)
            hm_p, c_p, n_p, m_p = _mlstm_mixer(mq, mk, mv, so, gt,
                                               jnp.zeros((bp, MLSTM_HEADS, MLSTM_HEAD_DIM, MLSTM_HEAD_DIM), F32),
                                               jnp.zeros((bp, MLSTM_HEADS, MLSTM_HEAD_DIM), F32),
                                               jnp.zeros((bp, MLSTM_HEADS), F32), mlstm_norm[j], t=MLSTM_PROMPT_CHUNK)
            odd_p.append((jnp.transpose(fkt.reshape(bp, n_heads, HEAD_DIM, sp), (0, 3, 1, 2)),
                          jnp.transpose(fvt.reshape(bp, n_heads, HEAD_DIM, sp), (0, 3, 1, 2)),
                          gt[..., :FOX_HEADS], c_p, n_p, m_p))

            fq, fk, fv, gt, mq, mk, mv, so = [a.reshape(bs, ss, -1) for a in _proj_odd(hs, gm, *wproj, tm=TM_PROJ)]
            oc_s = _fox_sample_mixer(fq, fk, fv, gt, cache_fox_k[j], cache_fox_v[j], cache_fox_logf[j])
            hm_s, c_s, n_s, m_s = _mlstm_mixer(mq, mk, mv, so, gt, state_mlstm_C[j], state_mlstm_n[j], state_mlstm_m[j],
                                               mlstm_norm[j], t=ss)
            odd_s.append((fk.reshape(bs, ss, n_heads, HEAD_DIM), fv.reshape(bs, ss, n_heads, HEAD_DIM),
                          gt[..., :FOX_HEADS], c_s, n_s, m_s))

            mix_p, mix_s = (oc_p, hm_p), (oc_s, hm_s)
            w_out = odd_w_out[j]

        ka = mix_p[0].shape[-1]
        wt = (bf(w_out[:ka]), bf(w_out[ka:]), _row(norm_ffn2[i]), bf(ffn2_wg[i]), bf(ffn2_wu[i]), bf(ffn2_wd[i]),
              bf(ple_proj[i]), _row(ple_norm[i]), _row(ple_gate_norm[i]), bf(ple_gate_w[i]))
        hp = _tail(hp, mix_p[0].reshape(bp * sp, -1), mix_p[1].reshape(bp * sp, -1),
                   p_prompt[i].reshape(bp * sp, PLE_DIM), *wt, tm=TM_PROJ)
        hs = _tail(hs, mix_s[0].reshape(bs * ss, -1), mix_s[1].reshape(bs * ss, -1),
                   p_sample[i].reshape(bs * ss, PLE_DIM), *wt, tm=TM_PROJ)

    def stk(states, idx):
        return jnp.stack([s[idx] for s in states])

    return (hp.reshape(bp, sp, D_MODEL), hs.reshape(bs, ss, D_MODEL),
            stk(even_p, 0), stk(even_s, 0), stk(even_p, 1), stk(even_s, 1),
            stk(even_p, 2), stk(even_s, 2), stk(even_p, 3), stk(even_s, 3),
            stk(odd_p, 0), stk(odd_s, 0), stk(odd_p, 1), stk(odd_s, 1), stk(odd_p, 2), stk(odd_s, 2),
            stk(odd_p, 3), stk(odd_s, 3), stk(odd_p, 4), stk(odd_s, 4), stk(odd_p, 5), stk(odd_s, 5))
```
